```python
import math
import jax
import jax.numpy as jnp
from jax import lax
import numpy as np

D_MODEL = 1024
BATCH = 4
SEQ = 4096
DEPTH = 4
DEC_BATCH = 32
DEC_SEQ = 1
PAST_LEN = 8192
PAGE_SIZE = 128

D_MIX = D_MODEL
D_ATTN = D_MIX // 2
N_DIFF_HEADS = 4
DIFF_HEAD_DIM = D_ATTN // N_DIFF_HEADS // 2
DIFF_V_DIM = 2 * DIFF_HEAD_DIM
D_SGU = D_MIX - D_ATTN
N_SGU_GROUPS = 4
SGU_GROUP_DIM = D_SGU // N_SGU_GROUPS
CHUNK = 128
D_IN_PROJ = 3 * D_ATTN + 2 * D_SGU
Q_BLOCK = 128
D_FF = 2816
N_EXPERTS = 8
TOP_K = 2
D_FF_EXPERT = 3584
N_DENSE_LAYERS = (DEPTH + 1) // 2
N_MOE_LAYERS = DEPTH // 2
EPS = 1e-6

kernel_name = 'hybrid_diffattn_sgu_decoder_step'


def rmsnorm(x, g):
    xf = x.astype(jnp.float32)
    y = xf * lax.rsqrt(jnp.mean(xf * xf, axis=-1, keepdims=True) + EPS)
    return (y * g.astype(jnp.float32)).astype(x.dtype)


def alibi_slopes(n):
    return jnp.asarray(2.0 ** (-8.0 * np.arange(1, n + 1) / n), dtype=jnp.float32)


def diff_attend(q, k, v, q_pos, k_pos, lam):
    s = jnp.einsum('bqhcd,bkhcd->cbhqk', q, k, preferred_element_type=jnp.float32) * (DIFF_HEAD_DIM ** -0.5)
    dist = (q_pos[:, None] - k_pos[None, :]).astype(jnp.float32)
    bias = jnp.where(dist >= 0.0, -alibi_slopes(N_DIFF_HEADS)[:, None, None] * dist, -jnp.inf)
    p = jax.nn.softmax(s + bias, axis=-1)
    w = p[0] - lam * p[1]
    return jnp.einsum('bhqk,bkhe->bqhe', w.astype(v.dtype), v)


def prompt_diff_attention(q, k, v, lam):
    b, s = q.shape[0], q.shape[1]
    nb = s // Q_BLOCK
    qb = q.reshape(b, nb, Q_BLOCK, N_DIFF_HEADS, 2, DIFF_HEAD_DIM).transpose(1, 0, 2, 3, 4, 5)
    pos = jnp.arange(s, dtype=jnp.int32)
    pb = pos.reshape(nb, Q_BLOCK)
    out = lax.map(lambda qp: diff_attend(qp[0], k, v, qp[1], pos, lam), (qb, pb))
    return out.transpose(1, 0, 2, 3, 4).reshape(b, s, N_DIFF_HEADS, DIFF_V_DIM)


def split_qk(k):
    return k.reshape(k.shape[0], k.shape[1], N_DIFF_HEADS, 2, DIFF_HEAD_DIM)


def chunk_mix(gv_chunks, w_s, b_s):
    n = gv_chunks.shape[2]
    mask = jnp.tril(jnp.ones((n, n), dtype=bool))
    w = jnp.where(mask[None], w_s[:, :n, :n], jnp.zeros((), w_s.dtype))
    out = jnp.einsum('gij,bcjge->bcige', w, gv_chunks)
    return out + b_s[:, :n].T[None, None, :, :, None]


def mixer_inputs(x, g_norm, w_in_l, g_sgu_l):
    b, L = x.shape[0], x.shape[1]
    h = rmsnorm(x, g_norm)
    z = h @ w_in_l
    q, k, v, u, gv = jnp.split(z, [D_ATTN, 2 * D_ATTN, 3 * D_ATTN, 3 * D_ATTN + D_SGU], axis=-1)
    q = q.reshape(b, L, N_DIFF_HEADS, 2, DIFF_HEAD_DIM)
    k = k.reshape(b, L, N_DIFF_HEADS, 2 * DIFF_HEAD_DIM)
    v = v.reshape(b, L, N_DIFF_HEADS, DIFF_V_DIM)
    u = jax.nn.gelu(u)
    gv = rmsnorm(jax.nn.gelu(gv), g_sgu_l).reshape(b, L, N_SGU_GROUPS, SGU_GROUP_DIM)
    return q, k, v, u, gv


def mixer_merge(x, o, u, mixed, subln_g, lam_init, w_out_l):
    b, L = x.shape[0], x.shape[1]
    a = (rmsnorm(o, subln_g) * (1.0 - lam_init)).reshape(b, L, D_ATTN)
    s = u * mixed.reshape(b, L, D_SGU)
    return x + jnp.concatenate([a, s], axis=-1) @ w_out_l


def swiglu(h, w_gate, w_up, w_down):
    return (jax.nn.silu(h @ w_gate) * (h @ w_up)) @ w_down


def moe_swiglu(h, w_router, w_gate, w_up, w_down):
    logits = jnp.einsum('bld,de->ble', h, w_router, preferred_element_type=jnp.float32)
    top_logit, top_idx = lax.top_k(logits, TOP_K)
    top_w = jax.nn.softmax(top_logit, axis=-1)
    gates = jnp.sum(jax.nn.one_hot(top_idx, N_EXPERTS, dtype=jnp.float32) * top_w[..., None], axis=-2)
    y = jnp.zeros_like(h)
    for e in range(N_EXPERTS):
        y = y + gates[..., e:e + 1].astype(h.dtype) * swiglu(h, w_gate[e], w_up[e], w_down[e])
    return y


def channel_mixer(x, l, g_ffn_norm, ffn_w_gate, ffn_w_up, ffn_w_down,
                  moe_w_router, moe_w_gate, moe_w_up, moe_w_down):
    h = rmsnorm(x, g_ffn_norm[l])
    i = l // 2
    if l % 2 == 0:
        return x + swiglu(h, ffn_w_gate[i], ffn_w_up[i], ffn_w_down[i])
    return x + moe_swiglu(h, moe_w_router[i], moe_w_gate[i], moe_w_up[i], moe_w_down[i])


def setup_inputs(seed: int = 0) -> dict:
    key = jax.random.key(seed)
    ks = jax.random.split(key, 26)
    f32 = jnp.float32
    n_pages = PAST_LEN // PAGE_SIZE
    n_used = DEC_BATCH * n_pages
    n_pool = n_used + n_used // 4

    def nrm(k, shape, scale):
        return jax.random.normal(k, shape, f32) * scale

    def gain(k, shape):
        return 1.0 + 0.02 * jax.random.normal(k, shape, f32)

    page_table = jax.random.permutation(ks[4], n_pool)[:n_used].reshape(DEC_BATCH, n_pages).astype(jnp.int32)
    return {
        'x_prompt': nrm(ks[0], (BATCH, SEQ, D_MODEL), 1.0),
        'x_sample': nrm(ks[1], (DEC_BATCH, DEC_SEQ, D_MODEL), 1.0),
        'cache_k': nrm(ks[2], (DEPTH, n_pool, PAGE_SIZE, N_DIFF_HEADS, 2 * DIFF_HEAD_DIM), 1.0),
        'cache_v': nrm(ks[3], (DEPTH, n_pool, PAGE_SIZE, N_DIFF_HEADS, DIFF_V_DIM), 1.0),
        'page_table': page_table,
        'w_in': nrm(ks[5], (DEPTH, D_MODEL, D_IN_PROJ), D_MODEL ** -0.5),
        'w_out': nrm(ks[6], (DEPTH, D_MIX, D_MODEL), D_MIX ** -0.5),
        'g_mix_norm': gain(ks[7], (DEPTH, D_MODEL)),
        'g_sgu_norm': gain(ks[8], (DEPTH, D_SGU)),
        'diff_subln_g': gain(ks[9], (DEPTH, DIFF_V_DIM)),
        'lambda_q1': nrm(ks[10], (DEPTH, DIFF_HEAD_DIM), 0.1),
        'lambda_k1': nrm(ks[11], (DEPTH, DIFF_HEAD_DIM), 0.1),
        'lambda_q2': nrm(ks[12], (DEPTH, DIFF_HEAD_DIM), 0.1),
        'lambda_k2': nrm(ks[13], (DEPTH, DIFF_HEAD_DIM), 0.1),
        'sgu_w': nrm(ks[14], (DEPTH, N_SGU_GROUPS, CHUNK, CHUNK), CHUNK ** -0.5),
        'sgu_b': gain(ks[15], (DEPTH, N_SGU_GROUPS, CHUNK)),
        'g_ffn_norm': gain(ks[16], (DEPTH, D_MODEL)),
        'ffn_w_gate': nrm(ks[17], (N_DENSE_LAYERS, D_MODEL, D_FF), D_MODEL ** -0.5),
        'ffn_w_up': nrm(ks[18], (N_DENSE_LAYERS, D_MODEL, D_FF), D_MODEL ** -0.5),
        'ffn_w_down': nrm(ks[19], (N_DENSE_LAYERS, D_FF, D_MODEL), D_FF ** -0.5),
        'moe_w_router': nrm(ks[20], (N_MOE_LAYERS, D_MODEL, N_EXPERTS), D_MODEL ** -0.5),
        'moe_w_gate': nrm(ks[21], (N_MOE_LAYERS, N_EXPERTS, D_MODEL, D_FF_EXPERT), D_MODEL ** -0.5),
        'moe_w_up': nrm(ks[22], (N_MOE_LAYERS, N_EXPERTS, D_MODEL, D_FF_EXPERT), D_MODEL ** -0.5),
        'moe_w_down': nrm(ks[23], (N_MOE_LAYERS, N_EXPERTS, D_FF_EXPERT, D_MODEL), D_FF_EXPERT ** -0.5),
        'g_final_norm': gain(ks[24], (D_MODEL,)),
    }


def reference(x_prompt, x_sample, cache_k, cache_v, page_table, w_in, w_out, g_mix_norm, g_sgu_norm,
              diff_subln_g, lambda_q1, lambda_k1, lambda_q2, lambda_k2, sgu_w, sgu_b, g_ffn_norm,
              ffn_w_gate, ffn_w_up, ffn_w_down, moe_w_router, moe_w_gate, moe_w_up, moe_w_down,
              g_final_norm):
    f32 = jnp.float32
    bp, sp = x_prompt.shape[0], x_prompt.shape[1]
    bs, ss = x_sample.shape[0], x_sample.shape[1]
    n_past = page_table.shape[1] * cache_k.shape[2]
    q_pos_s = n_past + jnp.arange(ss, dtype=jnp.int32)
    k_pos_s = jnp.arange(n_past + ss, dtype=jnp.int32)
    xp, xs = x_prompt, x_sample
    k_p, v_p, k_s, v_s, gv_s = [], [], [], [], []
    for l in range(DEPTH):
        lam_init = 0.8 - 0.6 * math.exp(-0.3 * l)
        lam = (jnp.exp(jnp.sum(lambda_q1[l].astype(f32) * lambda_k1[l].astype(f32)))
               - jnp.exp(jnp.sum(lambda_q2[l].astype(f32) * lambda_k2[l].astype(f32))) + lam_init)

        q, k, v, u, gv = mixer_inputs(xp, g_mix_norm[l], w_in[l], g_sgu_norm[l])
        o = prompt_diff_attention(q, split_qk(k), v, lam)
        mixed = chunk_mix(gv.reshape(bp, sp // CHUNK, CHUNK, N_SGU_GROUPS, SGU_GROUP_DIM), sgu_w[l], sgu_b[l])
        xp = mixer_merge(xp, o, u, mixed, diff_subln_g[l], lam_init, w_out[l])
        xp = channel_mixer(xp, l, g_ffn_norm, ffn_w_gate, ffn_w_up, ffn_w_down,
                           moe_w_router, moe_w_gate, moe_w_up, moe_w_down)
        k_p.append(k)
        v_p.append(v)

        q2, k2, v2, u2, gv2 = mixer_inputs(xs, g_mix_norm[l], w_in[l], g_sgu_norm[l])
        k_past = cache_k[l][page_table].reshape(bs, n_past, N_DIFF_HEADS, 2 * DIFF_HEAD_DIM)
        v_past = cache_v[l][page_table].reshape(bs, n_past, N_DIFF_HEADS, DIFF_V_DIM)
        k_all = jnp.concatenate([k_past, k2], axis=1)
        v_all = jnp.concatenate([v_past, v2], axis=1)
        o2 = diff_attend(q2, split_qk(k_all), v_all, q_pos_s, k_pos_s, lam)
        mixed2 = chunk_mix(gv2.reshape(bs, 1, ss, N_SGU_GROUPS, SGU_GROUP_DIM), sgu_w[l], sgu_b[l])
        xs = mixer_merge(xs, o2, u2, mixed2, diff_subln_g[l], lam_init, w_out[l])
        xs = channel_mixer(xs, l, g_ffn_norm, ffn_w_gate, ffn_w_up, ffn_w_down,
                           moe_w_router, moe_w_gate, moe_w_up, moe_w_down)
        k_s.append(k2)
        v_s.append(v2)
        gv_s.append(gv2.reshape(bs, ss, D_SGU))

    y_prompt = rmsnorm(xp, g_final_norm)
    y_sample = rmsnorm(xs, g_final_norm)
    return (y_prompt, y_sample, jnp.stack(k_p), jnp.stack(v_p), jnp.stack(k_s), jnp.stack(v_s), jnp.stack(gv_s))
```

```python
import functools
import math

import jax
import jax.numpy as jnp
import numpy as np
from jax import lax
from jax.experimental import pallas as pl
from jax.experimental.pallas import tpu as pltpu

F32 = jnp.float32
BF16 = jnp.bfloat16
EPS = 1e-6

N_DIFF_HEADS = 4
N_SGU_GROUPS = 4
CHUNK = 128
TOP_K = 2
LANES = 128
VMEM_LIMIT = 56 * 1024 * 1024

NT_DIMS = (((1,), (1,)), ((), ()))


def _rms(x, g):
    return x * lax.rsqrt(jnp.mean(x * x, axis=-1, keepdims=True) + EPS) * g


def _params(*sem):
    return pltpu.CompilerParams(dimension_semantics=sem, vmem_limit_bytes=VMEM_LIMIT)


def _in_proj_common(x_ref, g_ref, w_ref, gsgu_ref, d_attn, d_sgu):
    h = _rms(x_ref[...], g_ref[...]).astype(BF16)

    def proj(c0, c1):
        return jnp.dot(h, w_ref[:, c0:c1], preferred_element_type=F32)

    q = proj(0, d_attn) * (float(d_attn // N_DIFF_HEADS // 2) ** -0.5)
    k = proj(d_attn, 2 * d_attn)
    v = proj(2 * d_attn, 3 * d_attn)
    u = jax.nn.gelu(proj(3 * d_attn, 3 * d_attn + d_sgu))
    gv = _rms(jax.nn.gelu(proj(3 * d_attn + d_sgu, 3 * d_attn + 2 * d_sgu)), gsgu_ref[...])
    return q, k, v, u, gv


def _in_proj_prompt_kernel(x_ref, g_ref, w_ref, gsgu_ref, sw_ref, sb_ref,
                           q_ref, kf_ref, kb_ref, vf_ref, vb_ref, s_ref, *, d_attn, d_sgu):
    q, k, v, u, gv = _in_proj_common(x_ref, g_ref, w_ref, gsgu_ref, d_attn, d_sgu)
    q_ref[...] = q.astype(BF16)
    kf_ref[...] = k
    kb_ref[...] = k.astype(BF16)
    vf_ref[...] = v
    vb_ref[...] = v.astype(BF16)
    tm = x_ref.shape[0]
    ge = d_sgu // N_SGU_GROUPS
    row = lax.broadcasted_iota(jnp.int32, (CHUNK, CHUNK), 0)
    col = lax.broadcasted_iota(jnp.int32, (CHUNK, CHUNK), 1)
    gvb = gv.astype(BF16)
    for g in range(N_SGU_GROUPS):
        wt = jnp.where(col <= row, sw_ref[g], 0.0).astype(BF16)
        bg = sb_ref[:, g:g + 1]
        for c in range(tm // CHUNK):
            r0, c0 = c * CHUNK, g * ge
            mixed = jnp.dot(wt, gvb[r0:r0 + CHUNK, c0:c0 + ge], preferred_element_type=F32) + bg
            s_ref[r0:r0 + CHUNK, c0:c0 + ge] = (u[r0:r0 + CHUNK, c0:c0 + ge] * mixed).astype(BF16)


def _in_proj_single_kernel(x_ref, g_ref, w_ref, gsgu_ref, wd_ref, bd_ref,
                           q_ref, k_ref, v_ref, s_ref, gv_ref, *, d_attn, d_sgu):
    q, k, v, u, gv = _in_proj_common(x_ref, g_ref, w_ref, gsgu_ref, d_attn, d_sgu)
    q_ref[...] = q
    k_ref[...] = k
    v_ref[...] = v
    gv_ref[...] = gv
    s_ref[...] = u * (gv * wd_ref[...] + bd_ref[...])


def _in_proj_prompt(x, g, w, gsgu, sw, sb_t, tm):
    t, d = x.shape
    d_sgu = gsgu.shape[1]
    d_attn = (w.shape[1] - 2 * d_sgu) // 3
    row = lambda i: (i, 0)
    full2 = lambda i: (0, 0)
    o_f32 = jax.ShapeDtypeStruct((t, d_attn), F32)
    o_bf = jax.ShapeDtypeStruct((t, d_attn), BF16)
    return pl.pallas_call(
        functools.partial(_in_proj_prompt_kernel, d_attn=d_attn, d_sgu=d_sgu),
        grid=(t // tm,),
        in_specs=[pl.BlockSpec((tm, d), row), pl.BlockSpec((1, d), full2),
                  pl.BlockSpec(w.shape, full2), pl.BlockSpec((1, d_sgu), full2),
                  pl.BlockSpec(sw.shape, lambda i: (0, 0, 0)), pl.BlockSpec(sb_t.shape, full2)],
        out_specs=[pl.BlockSpec((tm, d_attn), row)] * 5 + [pl.BlockSpec((tm, d_sgu), row)],
        out_shape=[o_bf, o_f32, o_bf, o_f32, o_bf, jax.ShapeDtypeStruct((t, d_sgu), BF16)],
        compiler_params=_params("parallel"),
        name="in_proj_prompt",
    )(x, g, w, gsgu, sw, sb_t)


def _in_proj_single(x, g, w, gsgu, wd, bd):
    t, d = x.shape
    d_sgu = gsgu.shape[1]
    d_attn = (w.shape[1] - 2 * d_sgu) // 3
    full2 = lambda i: (0, 0)
    o_a = jax.ShapeDtypeStruct((t, d_attn), F32)
    o_s = jax.ShapeDtypeStruct((t, d_sgu), F32)
    return pl.pallas_call(
        functools.partial(_in_proj_single_kernel, d_attn=d_attn, d_sgu=d_sgu),
        grid=(1,),
        in_specs=[pl.BlockSpec(x.shape, full2), pl.BlockSpec((1, d), full2),
                  pl.BlockSpec(w.shape, full2), pl.BlockSpec((1, d_sgu), full2),
                  pl.BlockSpec((1, d_sgu), full2), pl.BlockSpec((1, d_sgu), full2)],
        out_specs=[pl.BlockSpec((t, d_attn), full2)] * 3 + [pl.BlockSpec((t, d_sgu), full2)] * 2,
        out_shape=[o_a, o_a, o_a, o_s, o_s],
        compiler_params=_params("arbitrary"),
        name="in_proj_single",
    )(x, g, w, gsgu, wd, bd)


def _lambda(lamp_ref, lam_init):
    lp = lamp_ref[...]
    return (jnp.exp(jnp.sum(lp[0:1] * lp[1:2], keepdims=True))
            - jnp.exp(jnp.sum(lp[2:3] * lp[3:4], keepdims=True)) + lam_init)


def _head_slope(h):
    slope = jnp.float32(0.0)
    for i in range(N_DIFF_HEADS):
        slope = jnp.where(h == i, jnp.float32(2.0 ** (-8.0 * (i + 1) / N_DIFF_HEADS)), slope)
    return slope


def _attn_prompt_kernel(lamp_ref, g_ref, q_ref, k_ref, v_ref, o_ref, m_ref, l_ref, acc_ref,
                        *, tq, lam_init):
    h = pl.program_id(1)
    qi = pl.program_id(2)
    slope = _head_slope(h)
    half = q_ref.shape[1] // 2
    q = q_ref[...]
    lane = lax.broadcasted_iota(jnp.int32, q.shape, 1)
    zero = jnp.zeros_like(q)
    qs = (jnp.where(lane < half, q, zero), jnp.where(lane >= half, q, zero))
    q0 = qi * tq

    m_ref[...] = jnp.full(m_ref.shape, -jnp.inf, F32)
    l_ref[...] = jnp.zeros(l_ref.shape, F32)
    acc_ref[...] = jnp.zeros(acc_ref.shape, F32)

    def block(kb, masked):
        k0 = pl.multiple_of(kb * tq, tq)
        k = k_ref[pl.ds(k0, tq), :]
        v = v_ref[pl.ds(k0, tq), :]
        kpos = k0 + lax.broadcasted_iota(jnp.int32, (1, tq), 1)
        bias = slope * (kpos - q0).astype(F32)
        if masked:
            qpos = q0 + lax.broadcasted_iota(jnp.int32, (tq, 1), 0)
            keep = kpos <= qpos
        for c in range(2):
            s = lax.dot_general(qs[c], k, NT_DIMS, preferred_element_type=F32) + bias
            if masked:
                s = jnp.where(keep, s, -jnp.inf)
            m_old = m_ref[c]
            m_new = jnp.maximum(m_old, jnp.max(s, axis=-1, keepdims=True))
            p = jnp.exp(s - m_new)
            alpha = jnp.exp(m_old - m_new)
            l_ref[c] = alpha * l_ref[c] + jnp.sum(p, axis=-1, keepdims=True)
            acc_ref[c] = alpha * acc_ref[c] + jnp.dot(p.astype(BF16), v, preferred_element_type=F32)
            m_ref[c] = m_new

    def body(kb, carry):
        block(kb, False)
        return carry

    lax.fori_loop(0, qi, body, 0)
    block(qi, True)

    lam = _lambda(lamp_ref, lam_init)
    o = acc_ref[0] / l_ref[0] - lam * (acc_ref[1] / l_ref[1])
    o_ref[...] = (_rms(o, g_ref[...]) * (1.0 - lam_init)).astype(BF16)


def _attn_prompt(lamp, g, q, k, v, batch, tq, lam_init):
    t, d_attn = q.shape
    s = t // batch
    dv = d_attn // N_DIFF_HEADS
    nq = s // tq
    return pl.pallas_call(
        functools.partial(_attn_prompt_kernel, tq=tq, lam_init=lam_init),
        grid=(batch, N_DIFF_HEADS, nq),
        in_specs=[pl.BlockSpec(lamp.shape, lambda b, h, i: (0, 0)),
                  pl.BlockSpec((1, dv), lambda b, h, i: (0, 0)),
                  pl.BlockSpec((tq, dv), lambda b, h, i: (b * nq + i, h)),
                  pl.BlockSpec((s, dv), lambda b, h, i: (b, h)),
                  pl.BlockSpec((s, dv), lambda b, h, i: (b, h))],
        out_specs=pl.BlockSpec((tq, dv), lambda b, h, i: (b * nq + i, h)),
        out_shape=jax.ShapeDtypeStruct((t, d_attn), BF16),
        scratch_shapes=[pltpu.VMEM((2, tq, 1), F32), pltpu.VMEM((2, tq, 1), F32),
                        pltpu.VMEM((2, tq, dv), F32)],
        compiler_params=_params("parallel", "parallel", "arbitrary"),
        name="attn_prompt",
    )(lamp, g, q, k, v)


def _attn_decode_kernel(pt_ref, lamp_ref, g_ref, q_ref, kn_ref, vn_ref, *rest,
                        pages, n_past, lam_init):
    del pt_ref
    k_refs, v_refs = rest[:pages], rest[pages:2 * pages]
    o_ref, m_ref, l_ref, acc_ref = rest[2 * pages:]
    j = pl.program_id(1)
    page = k_refs[0].shape[0]
    dv = q_ref.shape[1] // N_DIFF_HEADS
    half = dv // 2
    span = pages * page

    @pl.when(j == 0)
    def _():
        m_ref[...] = jnp.full(m_ref.shape, -jnp.inf, F32)
        l_ref[...] = jnp.zeros(l_ref.shape, F32)
        acc_ref[...] = jnp.zeros(acc_ref.shape, F32)

    rowi = lax.broadcasted_iota(jnp.int32, (8, dv), 0)
    lane = lax.broadcasted_iota(jnp.int32, (8, dv), 1)
    sel = ((rowi == 0) & (lane < half)) | ((rowi == 1) & (lane >= half))
    kpos = j * span + lax.broadcasted_iota(jnp.int32, (1, span), 1)
    dist = (n_past - kpos).astype(F32)

    def update(h, s, pv_fn):
        m_old = m_ref[h]
        m_new = jnp.maximum(m_old, jnp.max(s, axis=-1, keepdims=True))
        p = jnp.exp(s - m_new)
        alpha = jnp.exp(m_old - m_new)
        l_ref[h] = alpha * l_ref[h] + jnp.sum(p, axis=-1, keepdims=True)
        acc_ref[h] = alpha * acc_ref[h] + pv_fn(p)
        m_ref[h] = m_new

    for h in range(N_DIFF_HEADS):
        c0 = h * dv
        slope = 2.0 ** (-8.0 * (h + 1) / N_DIFF_HEADS)
        qh = jnp.where(sel, jnp.broadcast_to(q_ref[:, c0:c0 + dv], (8, dv)), 0.0)
        qb = qh.astype(BF16)
        s = jnp.concatenate(
            [lax.dot_general(qb, k_refs[i][:, c0:c0 + dv].astype(BF16), NT_DIMS,
                             preferred_element_type=F32) for i in range(pages)], axis=-1)
        s = s - slope * dist

        def pv(p, c0=c0):
            pb = p.astype(BF16)
            out = jnp.zeros((8, dv), F32)
            for i in range(pages):
                out = out + jnp.dot(pb[:, i * page:(i + 1) * page],
                                    v_refs[i][:, c0:c0 + dv].astype(BF16),
                                    preferred_element_type=F32)
            return out

        update(h, s, pv)

    @pl.when(j == pl.num_programs(1) - 1)
    def _():
        lam = _lambda(lamp_ref, lam_init)
        for h in range(N_DIFF_HEADS):
            c0 = h * dv
            qh = jnp.where(sel, jnp.broadcast_to(q_ref[:, c0:c0 + dv], (8, dv)), 0.0)
            s_new = jnp.sum(qh * kn_ref[:, c0:c0 + dv], axis=-1, keepdims=True)
            update(h, s_new, lambda p, c0=c0: p * vn_ref[:, c0:c0 + dv])
            a = acc_ref[h] / l_ref[h]
            o = a[0:1] - lam * a[1:2]
            o_ref[:, c0:c0 + dv] = _rms(o, g_ref[...]) * (1.0 - lam_init)


def _attn_decode(page_table, lamp, g, q, k_new, v_new, cache_k, cache_v, layer, pages, lam_init):
    bs, d_attn = q.shape
    n_pages = page_table.shape[1]
    page = cache_k.shape[2]
    dv = d_attn // N_DIFF_HEADS
    steps = n_pages // pages
    row3 = lambda b, j, pt: (b, 0, 0)

    def page_spec(i):
        return pl.BlockSpec((None, None, page, d_attn),
                            lambda b, j, pt, i=i: (layer, pt[b, j * pages + i], 0, 0))

    grid_spec = pltpu.PrefetchScalarGridSpec(
        num_scalar_prefetch=1,
        grid=(bs, steps),
        in_specs=[pl.BlockSpec(lamp.shape, lambda b, j, pt: (0, 0)),
                  pl.BlockSpec((1, dv), lambda b, j, pt: (0, 0)),
                  pl.BlockSpec((None, 1, d_attn), row3),
                  pl.BlockSpec((None, 1, d_attn), row3),
                  pl.BlockSpec((None, 1, d_attn), row3)]
                 + [page_spec(i) for i in range(pages)] * 2,
        out_specs=pl.BlockSpec((None, 1, d_attn), row3),
        scratch_shapes=[pltpu.VMEM((N_DIFF_HEADS, 8, 1), F32), pltpu.VMEM((N_DIFF_HEADS, 8, 1), F32),
                        pltpu.VMEM((N_DIFF_HEADS, 8, dv), F32)],
    )
    r3 = lambda a: a.reshape(bs, 1, d_attn)
    out = pl.pallas_call(
        functools.partial(_attn_decode_kernel, pages=pages, n_past=n_pages * page, lam_init=lam_init),
        grid_spec=grid_spec,
        out_shape=jax.ShapeDtypeStruct((bs, 1, d_attn), F32),
        compiler_params=_params("parallel", "arbitrary"),
        name="attn_decode",
    )(page_table, lamp, g, r3(q), r3(k_new), r3(v_new), *([cache_k] * pages), *([cache_v] * pages))
    return out.reshape(bs, d_attn)


def _top2_route(logits, n_experts):
    lane = lax.broadcasted_iota(jnp.int32, logits.shape, 1)
    big = jnp.int32(logits.shape[1])
    lg = jnp.where(lane < n_experts, logits, -jnp.inf)
    m1 = jnp.max(lg, axis=-1, keepdims=True)
    i1 = jnp.min(jnp.where(lg == m1, lane, big), axis=-1, keepdims=True)
    lg2 = jnp.where(lane == i1, -jnp.inf, lg)
    m2 = jnp.max(lg2, axis=-1, keepdims=True)
    i2 = jnp.min(jnp.where(lg2 == m2, lane, big), axis=-1, keepdims=True)
    e2 = jnp.exp(m2 - m1)
    den = 1.0 + e2
    packed = jnp.where(lane == 0, i1.astype(F32), 0.0)
    packed = jnp.where(lane == 1, i2.astype(F32), packed)
    packed = jnp.where(lane == 2, 1.0 / den, packed)
    packed = jnp.where(lane == 3, e2 / den, packed)
    return packed


def _out_proj_kernel(*refs, n_experts):
    if n_experts:
        a_ref, s_ref, x_ref, w_ref, g_ref, wr_ref, xo_ref, h_ref, r_ref = refs
    else:
        a_ref, s_ref, x_ref, w_ref, g_ref, xo_ref, h_ref = refs
    d_attn = a_ref.shape[1]
    x = (x_ref[...]
         + jnp.dot(a_ref[...].astype(BF16), w_ref[:d_attn, :], preferred_element_type=F32)
         + jnp.dot(s_ref[...].astype(BF16), w_ref[d_attn:, :], preferred_element_type=F32))
    xo_ref[...] = x
    h = _rms(x, g_ref[...])
    h_ref[...] = h.astype(BF16)
    if n_experts:
        logits = jnp.dot(h, wr_ref[...], preferred_element_type=F32, precision=lax.Precision.HIGHEST)
        r_ref[...] = _top2_route(logits, n_experts)


def _out_proj(a, s, x, w, g, wr, n_experts, tm):
    t, d = x.shape
    d_attn, d_sgu = a.shape[1], s.shape[1]
    row = lambda i: (i, 0)
    full2 = lambda i: (0, 0)
    in_specs = [pl.BlockSpec((tm, d_attn), row), pl.BlockSpec((tm, d_sgu), row),
                pl.BlockSpec((tm, d), row), pl.BlockSpec(w.shape, full2), pl.BlockSpec((1, d), full2)]
    out_specs = [pl.BlockSpec((tm, d), row), pl.BlockSpec((tm, d), row)]
    out_shape = [jax.ShapeDtypeStruct((t, d), F32), jax.ShapeDtypeStruct((t, d), BF16)]
    args = [a, s, x, w, g]
    if n_experts:
        in_specs.append(pl.BlockSpec(wr.shape, full2))
        out_specs.append(pl.BlockSpec((tm, LANES), row))
        out_shape.append(jax.ShapeDtypeStruct((t, LANES), F32))
        args.append(wr)
    return pl.pallas_call(
        functools.partial(_out_proj_kernel, n_experts=n_experts),
        grid=(t // tm,),
        in_specs=in_specs, out_specs=out_specs, out_shape=out_shape,
        compiler_params=_params("parallel"),
        name="out_proj",
    )(*args)


def _ffn_kernel(te_ref, nt_ref, x_ref, wg_ref, wu_ref, wd_ref, *rest, mode):
    del te_ref
    if mode == "residual":
        res_ref, o_ref, acc_ref = rest
    else:
        gate_ref, o_ref, acc_ref = rest
    i = pl.program_id(0)
    j = pl.program_id(1)

    @pl.when((i >= nt_ref[0]) & (j == pl.num_programs(1) - 1))
    def _():
        o_ref[...] = jnp.zeros(o_ref.shape, F32)

    @pl.when(i < nt_ref[0])
    def _():
        @pl.when(j == 0)
        def _():
            acc_ref[...] = jnp.zeros(acc_ref.shape, F32)

        x = x_ref[...]
        gt = jnp.dot(x, wg_ref[...], preferred_element_type=F32)
        up = jnp.dot(x, wu_ref[...], preferred_element_type=F32)
        act = (gt * jax.nn.sigmoid(gt) * up).astype(BF16)
        acc_ref[...] += jnp.dot(act, wd_ref[...], preferred_element_type=F32)

        @pl.when(j == pl.num_programs(1) - 1)
        def _():
            if mode == "residual":
                o_ref[...] = res_ref[...] + acc_ref[...]
            else:
                o_ref[...] = gate_ref[...] * acc_ref[...]


def _ffn(tile_expert, n_tiles, x, wg, wu, wd, extra, mode, tm, tf):
    r, d = x.shape
    f = wg.shape[2]
    n_row = r // tm

    def xi(i, j, te, nt):
        return (jnp.minimum(i, nt[0] - 1), 0)

    def ji(i, j, nt):
        return jnp.where(i < nt[0], j, f // tf - 1)

    grid_spec = pltpu.PrefetchScalarGridSpec(
        num_scalar_prefetch=2,
        grid=(n_row, f // tf),
        in_specs=[pl.BlockSpec((tm, d), xi),
                  pl.BlockSpec((None, d, tf), lambda i, j, te, nt: (te[i], 0, ji(i, j, nt))),
                  pl.BlockSpec((None, d, tf), lambda i, j, te, nt: (te[i], 0, ji(i, j, nt))),
                  pl.BlockSpec((None, tf, d), lambda i, j, te, nt: (te[i], ji(i, j, nt), 0)),
                  pl.BlockSpec((tm, extra.shape[1]), xi)],
        out_specs=pl.BlockSpec((tm, d), lambda i, j, te, nt: (i, 0)),
        scratch_shapes=[pltpu.VMEM((tm, d), F32)],
    )
    return pl.pallas_call(
        functools.partial(_ffn_kernel, mode=mode),
        grid_spec=grid_spec,
        out_shape=jax.ShapeDtypeStruct((r, d), F32),
        compiler_params=_params("arbitrary", "arbitrary"),
        name="ffn_" + mode,
    )(tile_expert, n_tiles, x, wg, wu, wd, extra)


def _combine_kernel(x_ref, y1_ref, y2_ref, g_ref, o_ref, *, final_norm):
    x = x_ref[...] + (y1_ref[...] + y2_ref[...])
    o_ref[...] = _rms(x, g_ref[...]) if final_norm else x


def _combine(x, y1, y2, g, final_norm, tm):
    t, d = x.shape
    row = lambda i: (i, 0)
    return pl.pallas_call(
        functools.partial(_combine_kernel, final_norm=final_norm),
        grid=(t // tm,),
        in_specs=[pl.BlockSpec((tm, d), row)] * 3 + [pl.BlockSpec((1, d), lambda i: (0, 0))],
        out_specs=pl.BlockSpec((tm, d), row),
        out_shape=jax.ShapeDtypeStruct((t, d), F32),
        compiler_params=_params("parallel"),
        name="combine",
    )(x, y1, y2, g)


def _route_plan(route, n_experts, tm):
    t = route.shape[0]
    n_rows = TOP_K * t
    n_tiles_max = n_rows // tm + n_experts
    flat_e = jnp.concatenate([route[:, 0], route[:, 1]]).astype(jnp.int32)
    flat_w = jnp.concatenate([route[:, 2], route[:, 3]])
    onehot = (flat_e[:, None] == jnp.arange(n_experts, dtype=jnp.int32)[None, :]).astype(jnp.int32)
    rank = jnp.sum((jnp.cumsum(onehot, axis=0) - onehot) * onehot, axis=1)
    counts = jnp.sum(onehot, axis=0)
    tiles = (counts + tm - 1) // tm
    tile_end = jnp.cumsum(tiles)
    pos = (tile_end - tiles)[flat_e] * tm + rank
    tile_expert = jnp.minimum(
        jnp.searchsorted(tile_end, jnp.arange(n_tiles_max, dtype=jnp.int32), side="right"),
        n_experts - 1).astype(jnp.int32)
    token = jnp.tile(jnp.arange(t, dtype=jnp.int32), TOP_K)
    row_token = jnp.zeros((n_tiles_max * tm,), jnp.int32).at[pos].set(token)
    row_gate = jnp.zeros((n_tiles_max * tm,), F32).at[pos].set(flat_w)
    return pos, tile_expert, tile_end[-1:].astype(jnp.int32), row_token, row_gate


def _pick_tile(n, target):
    t = min(n, target)
    while n % t:
        t //= 2
    return t


def _ff_tile(f, target):
    best = LANES
    for k in range(1, f // LANES + 1):
        if f % (k * LANES) == 0 and k * LANES <= target:
            best = k * LANES
    return best


def kernel(x_prompt, x_sample, cache_k, cache_v, page_table, w_in, w_out, g_mix_norm, g_sgu_norm, diff_subln_g, lambda_q1, lambda_k1, lambda_q2, lambda_k2, sgu_w, sgu_b, g_ffn_norm, ffn_w_gate, ffn_w_up, ffn_w_down, moe_w_router, moe_w_gate, moe_w_up, moe_w_down, g_final_norm):
    bp, sp, d = x_prompt.shape
    bs, ss, _ = x_sample.shape
    assert ss == 1, "the sample group decodes one token per sequence"
    depth = w_in.shape[0]
    assert depth % 2 == 0, "the last layer must be a routed layer (it applies the final norm)"
    d_sgu = g_sgu_norm.shape[1]
    d_attn = (w_in.shape[2] - 2 * d_sgu) // 3
    n_experts = moe_w_router.shape[2]
    n_pool, page = cache_k.shape[1], cache_k.shape[2]
    tp = bp * sp

    tm_rows = _pick_tile(sp, 512)
    tq = _pick_tile(sp, 512)
    tm_ffn = _pick_tile(tp, 1024)
    tm_moe = _pick_tile(tp, 1024)
    pages = _pick_tile(page_table.shape[1], 8)

    xp = x_prompt.reshape(tp, d)
    xs = x_sample.reshape(bs, d)
    ck = cache_k.reshape(depth, n_pool, page, d_attn)
    cv = cache_v.reshape(depth, n_pool, page, d_attn)
    row2 = lambda a: a.reshape(1, -1)
    g_final = row2(g_final_norm)
    zero1 = jnp.zeros((1,), jnp.int32)

    k_p, v_p, k_s, v_s, gv_s = [], [], [], [], []
    for l in range(depth):
        lam_init = 0.8 - 0.6 * math.exp(-0.3 * l)
        lamp = jnp.stack([lambda_q1[l], lambda_k1[l], lambda_q2[l], lambda_k2[l]]).astype(F32)
        w_in_l = w_in[l].astype(BF16)
        w_out_l = w_out[l].astype(BF16)
        g_mix, g_sgu, g_sub, g_ffn = row2(g_mix_norm[l]), row2(g_sgu_norm[l]), row2(diff_subln_g[l]), row2(g_ffn_norm[l])
        moe = l % 2 == 1
        i = l // 2
        if moe:
            wr = jnp.zeros((d, LANES), F32).at[:, :n_experts].set(moe_w_router[i])
            ne = n_experts
        else:
            wr, ne = None, 0

        q, kf, kb, vf, vb, s = _in_proj_prompt(xp, g_mix, w_in_l, g_sgu, sgu_w[l], sgu_b[l].T, tm_rows)
        a = _attn_prompt(lamp, g_sub, q, kb, vb, bp, tq, lam_init)
        outs_p = _out_proj(a, s, xp, w_out_l, g_ffn, wr, ne, tm_rows)
        k_p.append(kf)
        v_p.append(vf)

        ge = d_sgu // N_SGU_GROUPS
        wd = row2(jnp.repeat(sgu_w[l][:, 0, 0], ge))
        bd = row2(jnp.repeat(sgu_b[l][:, 0], ge))
        q2, k2, v2, s2, gv2 = _in_proj_single(xs, g_mix, w_in_l, g_sgu, wd, bd)
        a2 = _attn_decode(page_table, lamp, g_sub, q2, k2, v2, ck, cv, l, pages, lam_init)
        outs_s = _out_proj(a2, s2, xs, w_out_l, g_ffn, wr, ne, bs)
        k_s.append(k2)
        v_s.append(v2)
        gv_s.append(gv2)

        if not moe:
            wg = ffn_w_gate[i].astype(BF16)[None]
            wu = ffn_w_up[i].astype(BF16)[None]
            wdn = ffn_w_down[i].astype(BF16)[None]
            tf = _ff_tile(wg.shape[2], 1408)
            (xp_mid, hp), (xs_mid, hs) = outs_p, outs_s
            xp = _ffn(jnp.zeros((tp // tm_ffn,), jnp.int32), jnp.full((1,), tp // tm_ffn, jnp.int32),
                      hp, wg, wu, wdn, xp_mid, "residual", tm_ffn, tf)
            xs = _ffn(zero1, zero1 + 1, hs, wg, wu, wdn, xs_mid, "residual", bs, tf)
        else:
            (xp_mid, hp, rp), (xs_mid, hs, rs) = outs_p, outs_s
            wg = moe_w_gate[i].astype(BF16)
            wu = moe_w_up[i].astype(BF16)
            wdn = moe_w_down[i].astype(BF16)
            tf = _ff_tile(wg.shape[2], 896)
            h_all = jnp.concatenate([hp, hs], axis=0)
            route = jnp.concatenate([rp, rs], axis=0)
            t_all = tp + bs
            pos, tile_expert, n_tiles, row_token, row_gate = _route_plan(route, n_experts, tm_moe)
            x_sorted = jnp.take(h_all, row_token, axis=0)
            y_sorted = _ffn(tile_expert, n_tiles, x_sorted, wg, wu, wdn, row_gate[:, None], "gated", tm_moe, tf)
            y1 = jnp.take(y_sorted, pos[:t_all], axis=0)
            y2 = jnp.take(y_sorted, pos[t_all:], axis=0)
            final = l == depth - 1
            xp = _combine(xp_mid, y1[:tp], y2[:tp], g_final, final, tm_rows)
            xs = _combine(xs_mid, y1[tp:], y2[tp:], g_final, final, bs)

    nh = N_DIFF_HEADS
    dv = d_attn // nh
    return (xp.reshape(bp, sp, d), xs.reshape(bs, ss, d),
            jnp.stack(k_p).reshape(depth, bp, sp, nh, dv), jnp.stack(v_p).reshape(depth, bp, sp, nh, dv),
            jnp.stack(k_s).reshape(depth, bs, ss, nh, dv), jnp.stack(v_s).reshape(depth, bs, ss, nh, dv),
            jnp.stack(gv_s).reshape(depth, bs, ss, d_sgu))
```

```python
import functools
import math

import jax
import jax.numpy as jnp
import numpy as np
from jax import lax
from jax.experimental import pallas as pl
from jax.experimental.pallas import tpu as pltpu

F32 = jnp.float32
BF16 = jnp.bfloat16
EPS = 1e-6

N_DIFF_HEADS = 4
N_SGU_GROUPS = 4
CHUNK = 128
TOP_K = 2
LANES = 128
VMEM_LIMIT = 56 * 1024 * 1024

NT_DIMS = (((1,), (1,)), ((), ()))


def _rms(x, g):
    return x * lax.rsqrt(jnp.mean(x * x, axis=-1, keepdims=True) + EPS) * g


def _params(*sem):
    return pltpu.CompilerParams(dimension_semantics=sem, vmem_limit_bytes=VMEM_LIMIT)


def _in_proj_common(x_ref, g_ref, w_ref, gsgu_ref, d_attn, d_sgu):
    h = _rms(x_ref[...], g_ref[...]).astype(BF16)

    def proj(c0, c1):
        return jnp.dot(h, w_ref[:, c0:c1], preferred_element_type=F32)

    q = proj(0, d_attn) * (float(d_attn // N_DIFF_HEADS // 2) ** -0.5)
    k = proj(d_attn, 2 * d_attn)
    v = proj(2 * d_attn, 3 * d_attn)
    u = jax.nn.gelu(proj(3 * d_attn, 3 * d_attn + d_sgu))
    gv = _rms(jax.nn.gelu(proj(3 * d_attn + d_sgu, 3 * d_attn + 2 * d_sgu)), gsgu_ref[...])
    return q, k, v, u, gv


def _in_proj_prompt_kernel(x_ref, g_ref, w_ref, gsgu_ref, sw_ref, sb_ref,
                           q_ref, kf_ref, kb_ref, vf_ref, vb_ref, s_ref, *, d_attn, d_sgu):
    q, k, v, u, gv = _in_proj_common(x_ref, g_ref, w_ref, gsgu_ref, d_attn, d_sgu)
    q_ref[...] = q.T.astype(BF16)
    kf_ref[...] = k
    kb_ref[...] = k.astype(BF16)
    vf_ref[...] = v
    vb_ref[...] = v.T.astype(BF16)
    tm = x_ref.shape[0]
    ge = d_sgu // N_SGU_GROUPS
    row = lax.broadcasted_iota(jnp.int32, (CHUNK, CHUNK), 0)
    col = lax.broadcasted_iota(jnp.int32, (CHUNK, CHUNK), 1)
    gvb = gv.astype(BF16)
    for g in range(N_SGU_GROUPS):
        wt = jnp.where(col <= row, sw_ref[g], 0.0).astype(BF16)
        bg = sb_ref[:, g:g + 1]
        for c in range(tm // CHUNK):
            r0, c0 = c * CHUNK, g * ge
            mixed = jnp.dot(wt, gvb[r0:r0 + CHUNK, c0:c0 + ge], preferred_element_type=F32) + bg
            s_ref[r0:r0 + CHUNK, c0:c0 + ge] = (u[r0:r0 + CHUNK, c0:c0 + ge] * mixed).astype(BF16)


def _in_proj_single_kernel(x_ref, g_ref, w_ref, gsgu_ref, wd_ref, bd_ref,
                           q_ref, k_ref, v_ref, s_ref, gv_ref, *, d_attn, d_sgu):
    q, k, v, u, gv = _in_proj_common(x_ref, g_ref, w_ref, gsgu_ref, d_attn, d_sgu)
    q_ref[...] = q
    k_ref[...] = k
    v_ref[...] = v
    gv_ref[...] = gv
    s_ref[...] = u * (gv * wd_ref[...] + bd_ref[...])


def _in_proj_prompt(x, g, w, gsgu, sw, sb_t, tm):
    t, d = x.shape
    d_sgu = gsgu.shape[1]
    d_attn = (w.shape[1] - 2 * d_sgu) // 3
    row = lambda i: (i, 0)
    full2 = lambda i: (0, 0)
    o_f32 = jax.ShapeDtypeStruct((t, d_attn), F32)
    o_bf = jax.ShapeDtypeStruct((t, d_attn), BF16)
    o_t = jax.ShapeDtypeStruct((t // tm, d_attn, tm), BF16)
    spec = pl.BlockSpec((tm, d_attn), row)
    spec_t = pl.BlockSpec((None, d_attn, tm), lambda i: (i, 0, 0))
    return pl.pallas_call(
        functools.partial(_in_proj_prompt_kernel, d_attn=d_attn, d_sgu=d_sgu),
        grid=(t // tm,),
        in_specs=[pl.BlockSpec((tm, d), row), pl.BlockSpec((1, d), full2),
                  pl.BlockSpec(w.shape, full2), pl.BlockSpec((1, d_sgu), full2),
                  pl.BlockSpec(sw.shape, lambda i: (0, 0, 0)), pl.BlockSpec(sb_t.shape, full2)],
        out_specs=[spec_t, spec, spec, spec, spec_t, pl.BlockSpec((tm, d_sgu), row)],
        out_shape=[o_t, o_f32, o_bf, o_f32, o_t, jax.ShapeDtypeStruct((t, d_sgu), BF16)],
        compiler_params=_params("parallel"),
        name="in_proj_prompt",
    )(x, g, w, gsgu, sw, sb_t)


def _in_proj_single(x, g, w, gsgu, wd, bd):
    t, d = x.shape
    d_sgu = gsgu.shape[1]
    d_attn = (w.shape[1] - 2 * d_sgu) // 3
    full2 = lambda i: (0, 0)
    o_a = jax.ShapeDtypeStruct((t, d_attn), F32)
    o_s = jax.ShapeDtypeStruct((t, d_sgu), F32)
    return pl.pallas_call(
        functools.partial(_in_proj_single_kernel, d_attn=d_attn, d_sgu=d_sgu),
        grid=(1,),
        in_specs=[pl.BlockSpec(x.shape, full2), pl.BlockSpec((1, d), full2),
                  pl.BlockSpec(w.shape, full2), pl.BlockSpec((1, d_sgu), full2),
                  pl.BlockSpec((1, d_sgu), full2), pl.BlockSpec((1, d_sgu), full2)],
        out_specs=[pl.BlockSpec((t, d_attn), full2)] * 3 + [pl.BlockSpec((t, d_sgu), full2)] * 2,
        out_shape=[o_a, o_a, o_a, o_s, o_s],
        compiler_params=_params("arbitrary"),
        name="in_proj_single",
    )(x, g, w, gsgu, wd, bd)


def _lambda(lamp_ref, lam_init):
    lp = lamp_ref[...]
    return (jnp.exp(jnp.sum(lp[0:1] * lp[1:2], keepdims=True))
            - jnp.exp(jnp.sum(lp[2:3] * lp[3:4], keepdims=True)) + lam_init)


def _head_slope(h):
    slope = jnp.float32(0.0)
    for i in range(N_DIFF_HEADS):
        slope = jnp.where(h == i, jnp.float32(2.0 ** (-8.0 * (i + 1) / N_DIFF_HEADS)), slope)
    return slope


def _attn_prompt_kernel(lamp_ref, g_ref, qt_ref, k_ref, vt_ref, o_ref, m_ref, l_ref, acc_ref,
                        *, tq, lam_init):
    h = pl.program_id(1)
    qi = pl.program_id(2)
    slope = _head_slope(h)
    dv = qt_ref.shape[0]
    qt = qt_ref[...]
    feat = lax.broadcasted_iota(jnp.int32, qt.shape, 0)
    zero = jnp.zeros_like(qt)
    qts = (jnp.where(feat < dv // 2, qt, zero), jnp.where(feat >= dv // 2, qt, zero))
    q0 = qi * tq

    key = lax.broadcasted_iota(jnp.int32, (tq, LANES), 0).astype(F32) * slope
    base = jnp.concatenate([key] * (tq // LANES), axis=1)

    m_ref[...] = jnp.full(m_ref.shape, -jnp.inf, F32)
    l_ref[...] = jnp.zeros(l_ref.shape, F32)
    acc_ref[...] = jnp.zeros(acc_ref.shape, F32)

    def block(kb, masked):
        k0 = pl.multiple_of(kb * tq, tq)
        k = k_ref[pl.ds(k0, tq), :]
        vt = vt_ref[kb]
        shift = jnp.full((1, tq), k0 - q0, jnp.int32).astype(F32) * slope
        if masked:
            keep = (lax.broadcasted_iota(jnp.int32, (tq, tq), 0)
                    <= lax.broadcasted_iota(jnp.int32, (tq, tq), 1))
        for c in range(2):
            s = jnp.dot(k, qts[c], preferred_element_type=F32) + base
            if masked:
                s = jnp.where(keep, s, -jnp.inf)
            m_old = m_ref[c]
            m_new = jnp.maximum(m_old, jnp.max(s, axis=0, keepdims=True) + shift)
            p = jnp.exp(s - (m_new - shift))
            alpha = jnp.exp(m_old - m_new)
            l_ref[c] = alpha * l_ref[c] + jnp.sum(p, axis=0, keepdims=True)
            acc_ref[c] = alpha * acc_ref[c] + jnp.dot(vt, p.astype(BF16), preferred_element_type=F32)
            m_ref[c] = m_new

    def body(kb, carry):
        block(kb, False)
        return carry

    lax.fori_loop(0, qi, body, 0)
    block(qi, True)

    lam = _lambda(lamp_ref, lam_init)
    ot = acc_ref[0] * (1.0 / l_ref[0]) - lam * (acc_ref[1] * (1.0 / l_ref[1]))
    ot = ot * lax.rsqrt(jnp.mean(ot * ot, axis=0, keepdims=True) + EPS) * g_ref[...] * (1.0 - lam_init)
    o_ref[...] = ot.T.astype(BF16)


def _attn_prompt(lamp, g_col, qt, k, vt, batch, lam_init):
    nblk, d_attn, tq = qt.shape
    t = nblk * tq
    s = t // batch
    dv = d_attn // N_DIFF_HEADS
    nq = s // tq
    return pl.pallas_call(
        functools.partial(_attn_prompt_kernel, tq=tq, lam_init=lam_init),
        grid=(batch, N_DIFF_HEADS, nq),
        in_specs=[pl.BlockSpec(lamp.shape, lambda b, h, i: (0, 0)),
                  pl.BlockSpec((dv, 1), lambda b, h, i: (0, 0)),
                  pl.BlockSpec((None, dv, tq), lambda b, h, i: (b * nq + i, h, 0)),
                  pl.BlockSpec((s, dv), lambda b, h, i: (b, h)),
                  pl.BlockSpec((nq, dv, tq), lambda b, h, i: (b, h, 0))],
        out_specs=pl.BlockSpec((tq, dv), lambda b, h, i: (b * nq + i, h)),
        out_shape=jax.ShapeDtypeStruct((t, d_attn), BF16),
        scratch_shapes=[pltpu.VMEM((2, 1, tq), F32), pltpu.VMEM((2, 1, tq), F32),
                        pltpu.VMEM((2, dv, tq), F32)],
        compiler_params=_params("parallel", "parallel", "arbitrary"),
        name="attn_prompt",
    )(lamp, g_col, qt, k, vt)


def _attn_decode_kernel(pt_ref, lamp_ref, g_ref, q_ref, kn_ref, vn_ref, *rest,
                        pages, n_past, lam_init):
    del pt_ref
    k_refs, v_refs = rest[:pages], rest[pages:2 * pages]
    o_ref, m_ref, l_ref, acc_ref = rest[2 * pages:]
    nh = N_DIFF_HEADS
    j = pl.program_id(1)
    page = k_refs[0].shape[0] // nh
    dv = q_ref.shape[1] // nh
    span = pages * page
    rows = 2 * nh

    @pl.when(j == 0)
    def _():
        m_ref[...] = jnp.full(m_ref.shape, -jnp.inf, F32)
        l_ref[...] = jnp.zeros(l_ref.shape, F32)
        acc_ref[...] = jnp.zeros(acc_ref.shape, F32)

    def head_rows(shape):
        return lax.broadcasted_iota(jnp.int32, shape, 0) // 2

    def stack_heads(ref):
        hr = head_rows((rows, dv))
        out = jnp.zeros((rows, dv), F32)
        for h in range(nh):
            out = jnp.where(hr == h, jnp.broadcast_to(ref[:, h * dv:(h + 1) * dv], (rows, dv)), out)
        return out

    rowi = lax.broadcasted_iota(jnp.int32, (rows, dv), 0)
    lane = lax.broadcasted_iota(jnp.int32, (rows, dv), 1)
    q8 = jnp.where((lane >= dv // 2) == (rowi % 2 == 1), stack_heads(q_ref), 0.0)
    q8b = q8.astype(BF16)
    slope = jnp.zeros((rows, 1), F32)
    for h in range(nh):
        slope = jnp.where(head_rows((rows, 1)) == h, jnp.float32(2.0 ** (-8.0 * (h + 1) / nh)), slope)

    def update(s, pv_fn):
        m_old = m_ref[...]
        m_new = jnp.maximum(m_old, jnp.max(s, axis=-1, keepdims=True))
        p = jnp.exp(s - m_new)
        alpha = jnp.exp(m_old - m_new)
        l_ref[...] = alpha * l_ref[...] + jnp.sum(p, axis=-1, keepdims=True)
        acc_ref[...] = alpha * acc_ref[...] + pv_fn(p)
        m_ref[...] = m_new

    def head_page(ref, h):
        return ref[pl.ds(h, page, stride=nh), :].astype(BF16)

    hr_s = head_rows((rows, span))
    s = None
    for h in range(nh):
        sh = jnp.concatenate(
            [lax.dot_general(q8b, head_page(k_refs[i], h), NT_DIMS, preferred_element_type=F32)
             for i in range(pages)], axis=-1)
        s = sh if s is None else jnp.where(hr_s == h, sh, s)
    kpos = j * span + lax.broadcasted_iota(jnp.int32, (1, span), 1)
    s = s - slope * (n_past - kpos).astype(F32)

    def pv(p):
        pb = p.astype(BF16)
        hr_v = head_rows((rows, dv))
        out = None
        for h in range(nh):
            oh = jnp.zeros((rows, dv), F32)
            for i in range(pages):
                oh = oh + jnp.dot(pb[:, i * page:(i + 1) * page], head_page(v_refs[i], h),
                                  preferred_element_type=F32)
            out = oh if out is None else jnp.where(hr_v == h, oh, out)
        return out

    update(s, pv)

    @pl.when(j == pl.num_programs(1) - 1)
    def _():
        s_new = jnp.sum(q8 * stack_heads(kn_ref), axis=-1, keepdims=True)
        vn8 = stack_heads(vn_ref)
        update(s_new, lambda p: p * vn8)
        lam = _lambda(lamp_ref, lam_init)
        a = acc_ref[...] * (1.0 / l_ref[...])
        for h in range(nh):
            o = a[2 * h:2 * h + 1] - lam * a[2 * h + 1:2 * h + 2]
            o_ref[:, h * dv:(h + 1) * dv] = _rms(o, g_ref[...]) * (1.0 - lam_init)


def _attn_decode(page_table, lamp, g, q, k_new, v_new, cache_k, cache_v, layer, pages, lam_init):
    bs, d_attn = q.shape
    n_pages = page_table.shape[1]
    rows_per_page, dv = cache_k.shape[2], cache_k.shape[3]
    page = rows_per_page // N_DIFF_HEADS
    steps = n_pages // pages
    row3 = lambda b, j, pt: (b, 0, 0)

    def page_spec(i):
        return pl.BlockSpec((None, None, rows_per_page, dv),
                            lambda b, j, pt, i=i: (layer, pt[b, j * pages + i], 0, 0))

    grid_spec = pltpu.PrefetchScalarGridSpec(
        num_scalar_prefetch=1,
        grid=(bs, steps),
        in_specs=[pl.BlockSpec(lamp.shape, lambda b, j, pt: (0, 0)),
                  pl.BlockSpec((1, dv), lambda b, j, pt: (0, 0)),
                  pl.BlockSpec((None, 1, d_attn), row3),
                  pl.BlockSpec((None, 1, d_attn), row3),
                  pl.BlockSpec((None, 1, d_attn), row3)]
                 + [page_spec(i) for i in range(pages)] * 2,
        out_specs=pl.BlockSpec((None, 1, d_attn), row3),
        scratch_shapes=[pltpu.VMEM((2 * N_DIFF_HEADS, 1), F32), pltpu.VMEM((2 * N_DIFF_HEADS, 1), F32),
                        pltpu.VMEM((2 * N_DIFF_HEADS, dv), F32)],
    )
    r3 = lambda a: a.reshape(bs, 1, d_attn)
    out = pl.pallas_call(
        functools.partial(_attn_decode_kernel, pages=pages, n_past=n_pages * page, lam_init=lam_init),
        grid_spec=grid_spec,
        out_shape=jax.ShapeDtypeStruct((bs, 1, d_attn), F32),
        compiler_params=_params("parallel", "arbitrary"),
        name="attn_decode",
    )(page_table, lamp, g, r3(q), r3(k_new), r3(v_new), *([cache_k] * pages), *([cache_v] * pages))
    return out.reshape(bs, d_attn)


def _top2_route(logits, n_experts):
    lane = lax.broadcasted_iota(jnp.int32, logits.shape, 1)
    big = jnp.int32(logits.shape[1])
    lg = jnp.where(lane < n_experts, logits, -jnp.inf)
    m1 = jnp.max(lg, axis=-1, keepdims=True)
    i1 = jnp.min(jnp.where(lg == m1, lane, big), axis=-1, keepdims=True)
    lg2 = jnp.where(lane == i1, -jnp.inf, lg)
    m2 = jnp.max(lg2, axis=-1, keepdims=True)
    i2 = jnp.min(jnp.where(lg2 == m2, lane, big), axis=-1, keepdims=True)
    e2 = jnp.exp(m2 - m1)
    den = 1.0 + e2
    packed = jnp.where(lane == 0, i1.astype(F32), 0.0)
    packed = jnp.where(lane == 1, i2.astype(F32), packed)
    packed = jnp.where(lane == 2, 1.0 / den, packed)
    packed = jnp.where(lane == 3, e2 / den, packed)
    return packed


def _out_proj_kernel(*refs, n_experts):
    if n_experts:
        a_ref, s_ref, x_ref, w_ref, g_ref, wr_ref, xo_ref, h_ref, r_ref = refs
    else:
        a_ref, s_ref, x_ref, w_ref, g_ref, xo_ref, h_ref = refs
    d_attn = a_ref.shape[1]
    x = (x_ref[...]
         + jnp.dot(a_ref[...].astype(BF16), w_ref[:d_attn, :], preferred_element_type=F32)
         + jnp.dot(s_ref[...].astype(BF16), w_ref[d_attn:, :], preferred_element_type=F32))
    xo_ref[...] = x
    h = _rms(x, g_ref[...])
    h_ref[...] = h.astype(BF16)
    if n_experts:
        logits = jnp.dot(h, wr_ref[...], preferred_element_type=F32, precision=lax.Precision.HIGHEST)
        r_ref[...] = _top2_route(logits, n_experts)


def _out_proj(a, s, x, w, g, wr, n_experts, tm):
    t, d = x.shape
    d_attn, d_sgu = a.shape[1], s.shape[1]
    row = lambda i: (i, 0)
    full2 = lambda i: (0, 0)
    in_specs = [pl.BlockSpec((tm, d_attn), row), pl.BlockSpec((tm, d_sgu), row),
                pl.BlockSpec((tm, d), row), pl.BlockSpec(w.shape, full2), pl.BlockSpec((1, d), full2)]
    out_specs = [pl.BlockSpec((tm, d), row), pl.BlockSpec((tm, d), row)]
    out_shape = [jax.ShapeDtypeStruct((t, d), F32), jax.ShapeDtypeStruct((t, d), BF16)]
    args = [a, s, x, w, g]
    if n_experts:
        in_specs.append(pl.BlockSpec(wr.shape, full2))
        out_specs.append(pl.BlockSpec((tm, LANES), row))
        out_shape.append(jax.ShapeDtypeStruct((t, LANES), F32))
        args.append(wr)
    return pl.pallas_call(
        functools.partial(_out_proj_kernel, n_experts=n_experts),
        grid=(t // tm,),
        in_specs=in_specs, out_specs=out_specs, out_shape=out_shape,
        compiler_params=_params("parallel"),
        name="out_proj",
    )(*args)


def _ffn_kernel(te_ref, nt_ref, x_ref, wg_ref, wu_ref, wd_ref, *rest, mode):
    del te_ref
    if mode == "residual":
        res_ref, o_ref, acc_ref = rest
    else:
        gate_ref, o_ref, acc_ref = rest
    i = pl.program_id(0)
    j = pl.program_id(1)

    @pl.when((i >= nt_ref[0]) & (j == pl.num_programs(1) - 1))
    def _():
        o_ref[...] = jnp.zeros(o_ref.shape, F32)

    @pl.when(i < nt_ref[0])
    def _():
        @pl.when(j == 0)
        def _():
            acc_ref[...] = jnp.zeros(acc_ref.shape, F32)

        x = x_ref[...]
        gt = jnp.dot(x, wg_ref[...], preferred_element_type=F32)
        up = jnp.dot(x, wu_ref[...], preferred_element_type=F32)
        act = (gt * jax.nn.sigmoid(gt) * up).astype(BF16)
        acc_ref[...] += jnp.dot(act, wd_ref[...], preferred_element_type=F32)

        @pl.when(j == pl.num_programs(1) - 1)
        def _():
            if mode == "residual":
                o_ref[...] = res_ref[...] + acc_ref[...]
            else:
                o_ref[...] = gate_ref[...] * acc_ref[...]


def _ffn(tile_expert, n_tiles, x, wg, wu, wd, extra, mode, tm, tf):
    r, d = x.shape
    f = wg.shape[2]
    n_row = r // tm

    def xi(i, j, te, nt):
        return (jnp.minimum(i, nt[0] - 1), 0)

    def ji(i, j, nt):
        return jnp.where(i < nt[0], j, f // tf - 1)

    grid_spec = pltpu.PrefetchScalarGridSpec(
        num_scalar_prefetch=2,
        grid=(n_row, f // tf),
        in_specs=[pl.BlockSpec((tm, d), xi),
                  pl.BlockSpec((None, d, tf), lambda i, j, te, nt: (te[i], 0, ji(i, j, nt))),
                  pl.BlockSpec((None, d, tf), lambda i, j, te, nt: (te[i], 0, ji(i, j, nt))),
                  pl.BlockSpec((None, tf, d), lambda i, j, te, nt: (te[i], ji(i, j, nt), 0)),
                  pl.BlockSpec((tm, extra.shape[1]), xi)],
        out_specs=pl.BlockSpec((tm, d), lambda i, j, te, nt: (i, 0)),
        scratch_shapes=[pltpu.VMEM((tm, d), F32)],
    )
    return pl.pallas_call(
        functools.partial(_ffn_kernel, mode=mode),
        grid_spec=grid_spec,
        out_shape=jax.ShapeDtypeStruct((r, d), F32),
        compiler_params=_params("arbitrary", "arbitrary"),
        name="ffn_" + mode,
    )(tile_expert, n_tiles, x, wg, wu, wd, extra)


def _combine_kernel(x_ref, y1_ref, y2_ref, g_ref, o_ref, *, final_norm):
    x = x_ref[...] + (y1_ref[...] + y2_ref[...])
    o_ref[...] = _rms(x, g_ref[...]) if final_norm else x


def _combine(x, y1, y2, g, final_norm, tm):
    t, d = x.shape
    row = lambda i: (i, 0)
    return pl.pallas_call(
        functools.partial(_combine_kernel, final_norm=final_norm),
        grid=(t // tm,),
        in_specs=[pl.BlockSpec((tm, d), row)] * 3 + [pl.BlockSpec((1, d), lambda i: (0, 0))],
        out_specs=pl.BlockSpec((tm, d), row),
        out_shape=jax.ShapeDtypeStruct((t, d), F32),
        compiler_params=_params("parallel"),
        name="combine",
    )(x, y1, y2, g)


def _route_plan(route, n_experts, tm):
    t = route.shape[0]
    n_rows = TOP_K * t
    n_tiles_max = n_rows // tm + n_experts
    flat_e = jnp.concatenate([route[:, 0], route[:, 1]]).astype(jnp.int32)
    flat_w = jnp.concatenate([route[:, 2], route[:, 3]])
    onehot = (flat_e[:, None] == jnp.arange(n_experts, dtype=jnp.int32)[None, :]).astype(jnp.int32)
    rank = jnp.sum((jnp.cumsum(onehot, axis=0) - onehot) * onehot, axis=1)
    counts = jnp.sum(onehot, axis=0)
    tiles = (counts + tm - 1) // tm
    tile_end = jnp.cumsum(tiles)
    pos = (tile_end - tiles)[flat_e] * tm + rank
    tile_ids = jnp.arange(n_tiles_max, dtype=jnp.int32)
    tile_expert = jnp.minimum(jnp.sum((tile_end[None, :] <= tile_ids[:, None]).astype(jnp.int32), axis=1),
                              n_experts - 1)
    token = jnp.tile(jnp.arange(t, dtype=jnp.int32), TOP_K)
    row_token = jnp.zeros((n_tiles_max * tm,), jnp.int32).at[pos].set(token)
    row_gate = jnp.zeros((n_tiles_max * tm,), F32).at[pos].set(flat_w)
    return pos, tile_expert, tile_end[-1:].astype(jnp.int32), row_token, row_gate


def _pick_tile(n, target):
    t = min(n, target)
    while n % t:
        t //= 2
    return t


def _ff_tile(f, target):
    best = LANES
    for k in range(1, f // LANES + 1):
        if f % (k * LANES) == 0 and k * LANES <= target:
            best = k * LANES
    return best


def kernel(x_prompt, x_sample, cache_k, cache_v, page_table, w_in, w_out, g_mix_norm, g_sgu_norm, diff_subln_g, lambda_q1, lambda_k1, lambda_q2, lambda_k2, sgu_w, sgu_b, g_ffn_norm, ffn_w_gate, ffn_w_up, ffn_w_down, moe_w_router, moe_w_gate, moe_w_up, moe_w_down, g_final_norm):
    bp, sp, d = x_prompt.shape
    bs, ss, _ = x_sample.shape
    assert ss == 1, "the sample group decodes one token per sequence"
    depth = w_in.shape[0]
    assert depth % 2 == 0, "the last layer must be a routed layer (it applies the final norm)"
    d_sgu = g_sgu_norm.shape[1]
    d_attn = (w_in.shape[2] - 2 * d_sgu) // 3
    n_experts = moe_w_router.shape[2]
    n_pool, page = cache_k.shape[1], cache_k.shape[2]
    tp = bp * sp

    tm_rows = _pick_tile(sp, 512)
    tm_ffn = _pick_tile(tp, 1024)
    tm_moe = _pick_tile(tp, 1024)
    pages = _pick_tile(page_table.shape[1], 8)

    xp = x_prompt.reshape(tp, d)
    xs = x_sample.reshape(bs, d)
    dv = d_attn // N_DIFF_HEADS
    ck = cache_k.reshape(depth, n_pool, page * N_DIFF_HEADS, dv)
    cv = cache_v.reshape(depth, n_pool, page * N_DIFF_HEADS, dv)
    row2 = lambda a: a.reshape(1, -1)
    g_final = row2(g_final_norm)
    zero1 = jnp.zeros((1,), jnp.int32)

    k_p, v_p, k_s, v_s, gv_s = [], [], [], [], []
    for l in range(depth):
        lam_init = 0.8 - 0.6 * math.exp(-0.3 * l)
        lamp = jnp.stack([lambda_q1[l], lambda_k1[l], lambda_q2[l], lambda_k2[l]]).astype(F32)
        w_in_l = w_in[l].astype(BF16)
        w_out_l = w_out[l].astype(BF16)
        g_mix, g_sgu, g_sub, g_ffn = row2(g_mix_norm[l]), row2(g_sgu_norm[l]), row2(diff_subln_g[l]), row2(g_ffn_norm[l])
        moe = l % 2 == 1
        i = l // 2
        if moe:
            wr = jnp.zeros((d, LANES), F32).at[:, :n_experts].set(moe_w_router[i])
            ne = n_experts
        else:
            wr, ne = None, 0

        qt, kf, kb, vf, vt, s = _in_proj_prompt(xp, g_mix, w_in_l, g_sgu, sgu_w[l], sgu_b[l].T, tm_rows)
        a = _attn_prompt(lamp, g_sub.reshape(dv, 1), qt, kb, vt, bp, lam_init)
        outs_p = _out_proj(a, s, xp, w_out_l, g_ffn, wr, ne, tm_rows)
        k_p.append(kf)
        v_p.append(vf)

        ge = d_sgu // N_SGU_GROUPS
        wd = row2(jnp.repeat(sgu_w[l][:, 0, 0], ge))
        bd = row2(jnp.repeat(sgu_b[l][:, 0], ge))
        q2, k2, v2, s2, gv2 = _in_proj_single(xs, g_mix, w_in_l, g_sgu, wd, bd)
        a2 = _attn_decode(page_table, lamp, g_sub, q2, k2, v2, ck, cv, l, pages, lam_init)
        outs_s = _out_proj(a2, s2, xs, w_out_l, g_ffn, wr, ne, bs)
        k_s.append(k2)
        v_s.append(v2)
        gv_s.append(gv2)

        if not moe:
            wg = ffn_w_gate[i].astype(BF16)[None]
            wu = ffn_w_up[i].astype(BF16)[None]
            wdn = ffn_w_down[i].astype(BF16)[None]
            tf = _ff_tile(wg.shape[2], 1408)
            (xp_mid, hp), (xs_mid, hs) = outs_p, outs_s
            xp = _ffn(jnp.zeros((tp // tm_ffn,), jnp.int32), jnp.full((1,), tp // tm_ffn, jnp.int32),
                      hp, wg, wu, wdn, xp_mid, "residual", tm_ffn, tf)
            xs = _ffn(zero1, zero1 + 1, hs, wg, wu, wdn, xs_mid, "residual", bs, tf)
        else:
            (xp_mid, hp, rp), (xs_mid, hs, rs) = outs_p, outs_s
            wg = moe_w_gate[i].astype(BF16)
            wu = moe_w_up[i].astype(BF16)
            wdn = moe_w_down[i].astype(BF16)
            tf = _ff_tile(wg.shape[2], 896)
            h_all = jnp.concatenate([hp, hs], axis=0)
            route = jnp.concatenate([rp, rs], axis=0)
            t_all = tp + bs
            pos, tile_expert, n_tiles, row_token, row_gate = _route_plan(route, n_experts, tm_moe)
            x_sorted = jnp.take(h_all, row_token, axis=0)
            y_sorted = _ffn(tile_expert, n_tiles, x_sorted, wg, wu, wdn, row_gate[:, None], "gated", tm_moe, tf)
            y1 = jnp.take(y_sorted, pos[:t_all], axis=0)
            y2 = jnp.take(y_sorted, pos[t_all:], axis=0)
            final = l == depth - 1
            xp = _combine(xp_mid, y1[:tp], y2[:tp], g_final, final, tm_rows)
            xs = _combine(xs_mid, y1[tp:], y2[tp:], g_final, final, bs)

    nh = N_DIFF_HEADS
    return (xp.reshape(bp, sp, d), xs.reshape(bs, ss, d),
            jnp.stack(k_p).reshape(depth, bp, sp, nh, dv), jnp.stack(v_p).reshape(depth, bp, sp, nh, dv),
            jnp.stack(k_s).reshape(depth, bs, ss, nh, dv), jnp.stack(v_s).reshape(depth, bs, ss, nh, dv),
            jnp.stack(gv_s).reshape(depth, bs, ss, d_sgu))
```

```python
import functools
import math

import jax
import jax.numpy as jnp
import numpy as np
from jax import lax
from jax.experimental import pallas as pl
from jax.experimental.pallas import tpu as pltpu

F32 = jnp.float32
BF16 = jnp.bfloat16
EPS = 1e-6

N_DIFF_HEADS = 4
N_SGU_GROUPS = 4
CHUNK = 128
TOP_K = 2
LANES = 128
VMEM_LIMIT = 56 * 1024 * 1024

NT_DIMS = (((1,), (1,)), ((), ()))


def _rms(x, g):
    return x * lax.rsqrt(jnp.mean(x * x, axis=-1, keepdims=True) + EPS) * g


def _mm(a, w):
    if w.dtype == F32:
        return jnp.dot(a.astype(F32), w, preferred_element_type=F32, precision=lax.Precision.HIGHEST)
    return jnp.dot(a.astype(BF16), w, preferred_element_type=F32)


def _params(*sem):
    return pltpu.CompilerParams(dimension_semantics=sem, vmem_limit_bytes=VMEM_LIMIT)


def _in_proj_common(x_ref, g_ref, w_ref, gsgu_ref, d_attn, d_sgu):
    h = _rms(x_ref[...], g_ref[...])
    if w_ref.dtype == BF16:
        h = h.astype(BF16)

    def proj(c0, c1):
        return _mm(h, w_ref[:, c0:c1])

    q = proj(0, d_attn) * (float(d_attn // N_DIFF_HEADS // 2) ** -0.5)
    k = proj(d_attn, 2 * d_attn)
    v = proj(2 * d_attn, 3 * d_attn)
    u = jax.nn.gelu(proj(3 * d_attn, 3 * d_attn + d_sgu))
    gv = _rms(jax.nn.gelu(proj(3 * d_attn + d_sgu, 3 * d_attn + 2 * d_sgu)), gsgu_ref[...])
    return q, k, v, u, gv


def _in_proj_prompt_kernel(x_ref, g_ref, w_ref, gsgu_ref, sw_ref, sb_ref,
                           q_ref, kf_ref, kb_ref, vf_ref, vb_ref, s_ref, *, d_attn, d_sgu):
    q, k, v, u, gv = _in_proj_common(x_ref, g_ref, w_ref, gsgu_ref, d_attn, d_sgu)
    q_ref[...] = q.T.astype(BF16)
    kf_ref[...] = k
    kb_ref[...] = k.astype(BF16)
    vf_ref[...] = v
    vb_ref[...] = v.T.astype(BF16)
    tm = x_ref.shape[0]
    ge = d_sgu // N_SGU_GROUPS
    row = lax.broadcasted_iota(jnp.int32, (CHUNK, CHUNK), 0)
    col = lax.broadcasted_iota(jnp.int32, (CHUNK, CHUNK), 1)
    gvb = gv.astype(BF16)
    for g in range(N_SGU_GROUPS):
        wt = jnp.where(col <= row, sw_ref[g], 0.0).astype(BF16)
        bg = sb_ref[:, g:g + 1]
        for c in range(tm // CHUNK):
            r0, c0 = c * CHUNK, g * ge
            mixed = jnp.dot(wt, gvb[r0:r0 + CHUNK, c0:c0 + ge], preferred_element_type=F32) + bg
            s_ref[r0:r0 + CHUNK, c0:c0 + ge] = (u[r0:r0 + CHUNK, c0:c0 + ge] * mixed).astype(BF16)


def _in_proj_single_kernel(x_ref, g_ref, w_ref, gsgu_ref, wd_ref, bd_ref,
                           q_ref, k_ref, v_ref, s_ref, gv_ref, *, d_attn, d_sgu):
    q, k, v, u, gv = _in_proj_common(x_ref, g_ref, w_ref, gsgu_ref, d_attn, d_sgu)
    q_ref[...] = q
    k_ref[...] = k
    v_ref[...] = v
    gv_ref[...] = gv
    s_ref[...] = u * (gv * wd_ref[...] + bd_ref[...])


def _in_proj_prompt(x, g, w, gsgu, sw, sb_t, tm):
    t, d = x.shape
    d_sgu = gsgu.shape[1]
    d_attn = (w.shape[1] - 2 * d_sgu) // 3
    row = lambda i: (i, 0)
    full2 = lambda i: (0, 0)
    o_f32 = jax.ShapeDtypeStruct((t, d_attn), F32)
    o_bf = jax.ShapeDtypeStruct((t, d_attn), BF16)
    o_t = jax.ShapeDtypeStruct((t // tm, d_attn, tm), BF16)
    spec = pl.BlockSpec((tm, d_attn), row)
    spec_t = pl.BlockSpec((None, d_attn, tm), lambda i: (i, 0, 0))
    return pl.pallas_call(
        functools.partial(_in_proj_prompt_kernel, d_attn=d_attn, d_sgu=d_sgu),
        grid=(t // tm,),
        in_specs=[pl.BlockSpec((tm, d), row), pl.BlockSpec((1, d), full2),
                  pl.BlockSpec(w.shape, full2), pl.BlockSpec((1, d_sgu), full2),
                  pl.BlockSpec(sw.shape, lambda i: (0, 0, 0)), pl.BlockSpec(sb_t.shape, full2)],
        out_specs=[spec_t, spec, spec, spec, spec_t, pl.BlockSpec((tm, d_sgu), row)],
        out_shape=[o_t, o_f32, o_bf, o_f32, o_t, jax.ShapeDtypeStruct((t, d_sgu), BF16)],
        compiler_params=_params("parallel"),
        name="in_proj_prompt",
    )(x, g, w, gsgu, sw, sb_t)


def _in_proj_single(x, g, w, gsgu, wd, bd):
    t, d = x.shape
    d_sgu = gsgu.shape[1]
    d_attn = (w.shape[1] - 2 * d_sgu) // 3
    full2 = lambda i: (0, 0)
    o_a = jax.ShapeDtypeStruct((t, d_attn), F32)
    o_s = jax.ShapeDtypeStruct((t, d_sgu), F32)
    return pl.pallas_call(
        functools.partial(_in_proj_single_kernel, d_attn=d_attn, d_sgu=d_sgu),
        grid=(1,),
        in_specs=[pl.BlockSpec(x.shape, full2), pl.BlockSpec((1, d), full2),
                  pl.BlockSpec(w.shape, full2), pl.BlockSpec((1, d_sgu), full2),
                  pl.BlockSpec((1, d_sgu), full2), pl.BlockSpec((1, d_sgu), full2)],
        out_specs=[pl.BlockSpec((t, d_attn), full2)] * 3 + [pl.BlockSpec((t, d_sgu), full2)] * 2,
        out_shape=[o_a, o_a, o_a, o_s, o_s],
        compiler_params=_params("arbitrary"),
        name="in_proj_single",
    )(x, g, w, gsgu, wd, bd)


def _lambda(lamp_ref, lam_init):
    lp = lamp_ref[...]
    return (jnp.exp(jnp.sum(lp[0:1] * lp[1:2], keepdims=True))
            - jnp.exp(jnp.sum(lp[2:3] * lp[3:4], keepdims=True)) + lam_init)


def _head_slope(h):
    slope = jnp.float32(0.0)
    for i in range(N_DIFF_HEADS):
        slope = jnp.where(h == i, jnp.float32(2.0 ** (-8.0 * (i + 1) / N_DIFF_HEADS)), slope)
    return slope


def _attn_prompt_kernel(lamp_ref, g_ref, qt_ref, k_ref, vt_ref, o_ref, m_ref, l_ref, acc_ref,
                        *, tq, lam_init):
    h = pl.program_id(1)
    qi = pl.program_id(2)
    slope = _head_slope(h)
    dv = qt_ref.shape[0]
    qt = qt_ref[...]
    feat = lax.broadcasted_iota(jnp.int32, qt.shape, 0)
    zero = jnp.zeros_like(qt)
    qts = (jnp.where(feat < dv // 2, qt, zero), jnp.where(feat >= dv // 2, qt, zero))
    q0 = qi * tq

    key = lax.broadcasted_iota(jnp.int32, (tq, LANES), 0).astype(F32) * slope
    base = jnp.concatenate([key] * (tq // LANES), axis=1)

    m_ref[...] = jnp.full(m_ref.shape, -jnp.inf, F32)
    l_ref[...] = jnp.zeros(l_ref.shape, F32)
    acc_ref[...] = jnp.zeros(acc_ref.shape, F32)

    def block(kb, masked):
        k0 = pl.multiple_of(kb * tq, tq)
        k = k_ref[pl.ds(k0, tq), :]
        vt = vt_ref[kb]
        shift = jnp.full((1, tq), k0 - q0, jnp.int32).astype(F32) * slope
        if masked:
            keep = (lax.broadcasted_iota(jnp.int32, (tq, tq), 0)
                    <= lax.broadcasted_iota(jnp.int32, (tq, tq), 1))
        for c in range(2):
            s = jnp.dot(k, qts[c], preferred_element_type=F32) + base
            if masked:
                s = jnp.where(keep, s, -jnp.inf)
            m_old = m_ref[c]
            m_new = jnp.maximum(m_old, jnp.max(s, axis=0, keepdims=True) + shift)
            p = jnp.exp(s - (m_new - shift))
            alpha = jnp.exp(m_old - m_new)
            l_ref[c] = alpha * l_ref[c] + jnp.sum(p, axis=0, keepdims=True)
            acc_ref[c] = alpha * acc_ref[c] + jnp.dot(vt, p.astype(BF16), preferred_element_type=F32)
            m_ref[c] = m_new

    def body(kb, carry):
        block(kb, False)
        return carry

    lax.fori_loop(0, qi, body, 0)
    block(qi, True)

    lam = _lambda(lamp_ref, lam_init)
    ot = acc_ref[0] * (1.0 / l_ref[0]) - lam * (acc_ref[1] * (1.0 / l_ref[1]))
    ot = ot * lax.rsqrt(jnp.mean(ot * ot, axis=0, keepdims=True) + EPS) * g_ref[...] * (1.0 - lam_init)
    o_ref[...] = ot.T.astype(BF16)


def _attn_prompt(lamp, g_col, qt, k, vt, batch, lam_init):
    nblk, d_attn, tq = qt.shape
    t = nblk * tq
    s = t // batch
    dv = d_attn // N_DIFF_HEADS
    nq = s // tq
    return pl.pallas_call(
        functools.partial(_attn_prompt_kernel, tq=tq, lam_init=lam_init),
        grid=(batch, N_DIFF_HEADS, nq),
        in_specs=[pl.BlockSpec(lamp.shape, lambda b, h, i: (0, 0)),
                  pl.BlockSpec((dv, 1), lambda b, h, i: (0, 0)),
                  pl.BlockSpec((None, dv, tq), lambda b, h, i: (b * nq + i, h, 0)),
                  pl.BlockSpec((s, dv), lambda b, h, i: (b, h)),
                  pl.BlockSpec((nq, dv, tq), lambda b, h, i: (b, h, 0))],
        out_specs=pl.BlockSpec((tq, dv), lambda b, h, i: (b * nq + i, h)),
        out_shape=jax.ShapeDtypeStruct((t, d_attn), BF16),
        scratch_shapes=[pltpu.VMEM((2, 1, tq), F32), pltpu.VMEM((2, 1, tq), F32),
                        pltpu.VMEM((2, dv, tq), F32)],
        compiler_params=_params("parallel", "parallel", "arbitrary"),
        name="attn_prompt",
    )(lamp, g_col, qt, k, vt)


def _attn_decode_kernel(pt_ref, lamp_ref, g_ref, q_ref, kn_ref, vn_ref, *rest,
                        pages, n_past, lam_init):
    del pt_ref
    k_refs, v_refs = rest[:pages], rest[pages:2 * pages]
    o_ref, m_ref, l_ref, acc_ref = rest[2 * pages:]
    nh = N_DIFF_HEADS
    j = pl.program_id(1)
    page = k_refs[0].shape[0] // nh
    dv = q_ref.shape[1] // nh
    span = pages * page
    rows = 2 * nh

    @pl.when(j == 0)
    def _():
        m_ref[...] = jnp.full(m_ref.shape, -jnp.inf, F32)
        l_ref[...] = jnp.zeros(l_ref.shape, F32)
        acc_ref[...] = jnp.zeros(acc_ref.shape, F32)

    def head_rows(shape):
        return lax.broadcasted_iota(jnp.int32, shape, 0) // 2

    def stack_heads(ref):
        hr = head_rows((rows, dv))
        out = jnp.zeros((rows, dv), F32)
        for h in range(nh):
            out = jnp.where(hr == h, jnp.broadcast_to(ref[:, h * dv:(h + 1) * dv], (rows, dv)), out)
        return out

    rowi = lax.broadcasted_iota(jnp.int32, (rows, dv), 0)
    lane = lax.broadcasted_iota(jnp.int32, (rows, dv), 1)
    q8 = jnp.where((lane >= dv // 2) == (rowi % 2 == 1), stack_heads(q_ref), 0.0)
    q8b = q8.astype(BF16)
    slope = jnp.zeros((rows, 1), F32)
    for h in range(nh):
        slope = jnp.where(head_rows((rows, 1)) == h, jnp.float32(2.0 ** (-8.0 * (h + 1) / nh)), slope)

    def update(s, pv_fn):
        m_old = m_ref[...]
        m_new = jnp.maximum(m_old, jnp.max(s, axis=-1, keepdims=True))
        p = jnp.exp(s - m_new)
        alpha = jnp.exp(m_old - m_new)
        l_ref[...] = alpha * l_ref[...] + jnp.sum(p, axis=-1, keepdims=True)
        acc_ref[...] = alpha * acc_ref[...] + pv_fn(p)
        m_ref[...] = m_new

    def head_page(ref, h):
        return ref[pl.ds(h, page, stride=nh), :].astype(BF16)

    hr_s = head_rows((rows, span))
    s = None
    for h in range(nh):
        sh = jnp.concatenate(
            [lax.dot_general(q8b, head_page(k_refs[i], h), NT_DIMS, preferred_element_type=F32)
             for i in range(pages)], axis=-1)
        s = sh if s is None else jnp.where(hr_s == h, sh, s)
    kpos = j * span + lax.broadcasted_iota(jnp.int32, (1, span), 1)
    s = s - slope * (n_past - kpos).astype(F32)

    def pv(p):
        pb = p.astype(BF16)
        hr_v = head_rows((rows, dv))
        out = None
        for h in range(nh):
            oh = jnp.zeros((rows, dv), F32)
            for i in range(pages):
                oh = oh + jnp.dot(pb[:, i * page:(i + 1) * page], head_page(v_refs[i], h),
                                  preferred_element_type=F32)
            out = oh if out is None else jnp.where(hr_v == h, oh, out)
        return out

    update(s, pv)

    @pl.when(j == pl.num_programs(1) - 1)
    def _():
        s_new = jnp.sum(q8 * stack_heads(kn_ref), axis=-1, keepdims=True)
        vn8 = stack_heads(vn_ref)
        update(s_new, lambda p: p * vn8)
        lam = _lambda(lamp_ref, lam_init)
        a = acc_ref[...] * (1.0 / l_ref[...])
        for h in range(nh):
            o = a[2 * h:2 * h + 1] - lam * a[2 * h + 1:2 * h + 2]
            o_ref[:, h * dv:(h + 1) * dv] = _rms(o, g_ref[...]) * (1.0 - lam_init)


def _attn_decode(page_table, lamp, g, q, k_new, v_new, cache_k, cache_v, layer, pages, lam_init):
    bs, d_attn = q.shape
    n_pages = page_table.shape[1]
    rows_per_page, dv = cache_k.shape[2], cache_k.shape[3]
    page = rows_per_page // N_DIFF_HEADS
    steps = n_pages // pages
    row3 = lambda b, j, pt: (b, 0, 0)

    def page_spec(i):
        return pl.BlockSpec((None, None, rows_per_page, dv),
                            lambda b, j, pt, i=i: (layer, pt[b, j * pages + i], 0, 0))

    grid_spec = pltpu.PrefetchScalarGridSpec(
        num_scalar_prefetch=1,
        grid=(bs, steps),
        in_specs=[pl.BlockSpec(lamp.shape, lambda b, j, pt: (0, 0)),
                  pl.BlockSpec((1, dv), lambda b, j, pt: (0, 0)),
                  pl.BlockSpec((None, 1, d_attn), row3),
                  pl.BlockSpec((None, 1, d_attn), row3),
                  pl.BlockSpec((None, 1, d_attn), row3)]
                 + [page_spec(i) for i in range(pages)] * 2,
        out_specs=pl.BlockSpec((None, 1, d_attn), row3),
        scratch_shapes=[pltpu.VMEM((2 * N_DIFF_HEADS, 1), F32), pltpu.VMEM((2 * N_DIFF_HEADS, 1), F32),
                        pltpu.VMEM((2 * N_DIFF_HEADS, dv), F32)],
    )
    r3 = lambda a: a.reshape(bs, 1, d_attn)
    out = pl.pallas_call(
        functools.partial(_attn_decode_kernel, pages=pages, n_past=n_pages * page, lam_init=lam_init),
        grid_spec=grid_spec,
        out_shape=jax.ShapeDtypeStruct((bs, 1, d_attn), F32),
        compiler_params=_params("parallel", "arbitrary"),
        name="attn_decode",
    )(page_table, lamp, g, r3(q), r3(k_new), r3(v_new), *([cache_k] * pages), *([cache_v] * pages))
    return out.reshape(bs, d_attn)


def _top2_route(logits, n_experts):
    lane = lax.broadcasted_iota(jnp.int32, logits.shape, 1)
    big = jnp.int32(logits.shape[1])
    lg = jnp.where(lane < n_experts, logits, -jnp.inf)
    m1 = jnp.max(lg, axis=-1, keepdims=True)
    i1 = jnp.min(jnp.where(lg == m1, lane, big), axis=-1, keepdims=True)
    lg2 = jnp.where(lane == i1, -jnp.inf, lg)
    m2 = jnp.max(lg2, axis=-1, keepdims=True)
    i2 = jnp.min(jnp.where(lg2 == m2, lane, big), axis=-1, keepdims=True)
    e2 = jnp.exp(m2 - m1)
    den = 1.0 + e2
    packed = jnp.where(lane == 0, i1.astype(F32), 0.0)
    packed = jnp.where(lane == 1, i2.astype(F32), packed)
    packed = jnp.where(lane == 2, 1.0 / den, packed)
    packed = jnp.where(lane == 3, e2 / den, packed)
    return packed


def _out_proj_kernel(*refs, n_experts):
    if n_experts:
        a_ref, s_ref, x_ref, w_ref, g_ref, wr_ref, xo_ref, h_ref, r_ref = refs
    else:
        a_ref, s_ref, x_ref, w_ref, g_ref, xo_ref, h_ref = refs
    d_attn = a_ref.shape[1]
    x = x_ref[...] + _mm(a_ref[...], w_ref[:d_attn, :]) + _mm(s_ref[...], w_ref[d_attn:, :])
    xo_ref[...] = x
    h = _rms(x, g_ref[...])
    h_ref[...] = h.astype(h_ref.dtype)
    if n_experts:
        if wr_ref.dtype == F32:
            logits = _mm(h, wr_ref[0] + wr_ref[1])
        else:
            h_hi = h.astype(BF16)
            h_lo = (h - h_hi.astype(F32)).astype(BF16)
            logits = (jnp.dot(h_hi, wr_ref[0], preferred_element_type=F32)
                      + (jnp.dot(h_lo, wr_ref[0], preferred_element_type=F32)
                         + jnp.dot(h_hi, wr_ref[1], preferred_element_type=F32)))
        r_ref[...] = _top2_route(logits, n_experts)


def _out_proj(a, s, x, w, g, wr, n_experts, tm):
    t, d = x.shape
    d_attn, d_sgu = a.shape[1], s.shape[1]
    row = lambda i: (i, 0)
    full2 = lambda i: (0, 0)
    in_specs = [pl.BlockSpec((tm, d_attn), row), pl.BlockSpec((tm, d_sgu), row),
                pl.BlockSpec((tm, d), row), pl.BlockSpec(w.shape, full2), pl.BlockSpec((1, d), full2)]
    out_specs = [pl.BlockSpec((tm, d), row), pl.BlockSpec((tm, d), row)]
    out_shape = [jax.ShapeDtypeStruct((t, d), F32), jax.ShapeDtypeStruct((t, d), BF16 if (w.dtype == BF16 and not n_experts) else F32)]
    args = [a, s, x, w, g]
    if n_experts:
        in_specs.append(pl.BlockSpec(wr.shape, lambda i: (0, 0, 0)))
        out_specs.append(pl.BlockSpec((tm, LANES), row))
        out_shape.append(jax.ShapeDtypeStruct((t, LANES), F32))
        args.append(wr)
    return pl.pallas_call(
        functools.partial(_out_proj_kernel, n_experts=n_experts),
        grid=(t // tm,),
        in_specs=in_specs, out_specs=out_specs, out_shape=out_shape,
        compiler_params=_params("parallel"),
        name="out_proj",
    )(*args)


def _ffn_kernel(te_ref, nt_ref, x_ref, wg_ref, wu_ref, wd_ref, *rest, residual, cast_x):
    del te_ref
    rest = list(rest)
    res_ref = rest.pop(0) if residual else None
    o_ref, acc_ref = rest[0], rest[1]
    xb_ref = rest[2] if cast_x else x_ref
    i = pl.program_id(0)
    j = pl.program_id(1)

    @pl.when((i >= nt_ref[0]) & (j == pl.num_programs(1) - 1))
    def _():
        o_ref[...] = jnp.zeros(o_ref.shape, F32)

    @pl.when(i < nt_ref[0])
    def _():
        @pl.when(j == 0)
        def _():
            acc_ref[...] = jnp.zeros(acc_ref.shape, F32)
            if cast_x:
                xb_ref[...] = x_ref[...].astype(BF16)

        x = xb_ref[...]
        gt = _mm(x, wg_ref[...])
        up = _mm(x, wu_ref[...])
        acc_ref[...] += _mm(gt * jax.nn.sigmoid(gt) * up, wd_ref[...])

        @pl.when(j == pl.num_programs(1) - 1)
        def _():
            o_ref[...] = res_ref[...] + acc_ref[...] if residual else acc_ref[...]


def _ffn(tile_expert, n_tiles, x, wg, wu, wd, res, tm, tf):
    r, d = x.shape
    f = wg.shape[2]
    residual = res is not None
    cast_x = x.dtype != BF16 and wg.dtype == BF16

    def xi(i, j, te, nt):
        return (jnp.minimum(i, nt[0] - 1), 0)

    def ji(i, j, nt):
        return jnp.where(i < nt[0], j, f // tf - 1)

    in_specs = [pl.BlockSpec((tm, d), xi),
                pl.BlockSpec((None, d, tf), lambda i, j, te, nt: (te[i], 0, ji(i, j, nt))),
                pl.BlockSpec((None, d, tf), lambda i, j, te, nt: (te[i], 0, ji(i, j, nt))),
                pl.BlockSpec((None, tf, d), lambda i, j, te, nt: (te[i], ji(i, j, nt), 0))]
    args = [tile_expert, n_tiles, x, wg, wu, wd]
    if residual:
        in_specs.append(pl.BlockSpec((tm, d), xi))
        args.append(res)
    scratch = [pltpu.VMEM((tm, d), F32)] + ([pltpu.VMEM((tm, d), BF16)] if cast_x else [])
    grid_spec = pltpu.PrefetchScalarGridSpec(
        num_scalar_prefetch=2,
        grid=(r // tm, f // tf),
        in_specs=in_specs,
        out_specs=pl.BlockSpec((tm, d), lambda i, j, te, nt: (i, 0)),
        scratch_shapes=scratch,
    )
    return pl.pallas_call(
        functools.partial(_ffn_kernel, residual=residual, cast_x=cast_x),
        grid_spec=grid_spec,
        out_shape=jax.ShapeDtypeStruct((r, d), F32),
        compiler_params=_params("arbitrary", "arbitrary"),
        name="ffn_residual" if residual else "ffn_routed",
    )(*args)


def _rank_kernel(r_ref, rank_ref, cnt_ref, carry_ref):
    @pl.when(pl.program_id(0) == 0)
    def _():
        carry_ref[...] = jnp.zeros(carry_ref.shape, F32)

    r = r_ref[...]
    tt = r.shape[0]
    lane = lax.broadcasted_iota(jnp.int32, (tt, LANES), 1)
    lane_f = lane.astype(F32)
    oh1 = lane_f == r[:, 0:1]
    oh2 = lane_f == r[:, 1:2]
    chosen = jnp.where(oh1, 1.0, jnp.where(oh2, 1.0, 0.0))
    earlier = jnp.where(lax.broadcasted_iota(jnp.int32, (tt, tt), 1)
                        < lax.broadcasted_iota(jnp.int32, (tt, tt), 0), 1.0, 0.0).astype(BF16)
    before = jnp.dot(earlier, chosen.astype(BF16), preferred_element_type=F32) + carry_ref[...]
    rank1 = jnp.sum(jnp.where(oh1, before, 0.0), axis=-1, keepdims=True)
    rank2 = jnp.sum(jnp.where(oh2, before, 0.0), axis=-1, keepdims=True)
    rank_ref[...] = jnp.where(lane == 0, rank1, jnp.where(lane == 1, rank2, 0.0))
    carry_ref[...] += jnp.sum(chosen, axis=0, keepdims=True)
    cnt_ref[...] = carry_ref[...]


def _rank(route, tt):
    t = route.shape[0]
    return pl.pallas_call(
        _rank_kernel,
        grid=(t // tt,),
        in_specs=[pl.BlockSpec((tt, LANES), lambda i: (i, 0))],
        out_specs=[pl.BlockSpec((tt, LANES), lambda i: (i, 0)), pl.BlockSpec((1, LANES), lambda i: (0, 0))],
        out_shape=[jax.ShapeDtypeStruct((t, LANES), F32), jax.ShapeDtypeStruct((1, LANES), F32)],
        scratch_shapes=[pltpu.VMEM((1, LANES), F32)],
        compiler_params=_params("arbitrary"),
        name="route_rank",
    )(route)


def _dispatch_kernel(pos_ref, h_hbm, xs_in_hbm, xs_hbm, sem, *, tt):
    del xs_in_hbm
    base = pl.program_id(0) * tt

    def row_copy(r, slot):
        return pltpu.make_async_copy(h_hbm.at[pl.ds(base + r, 1)],
                                     xs_hbm.at[pl.ds(pos_ref[0, slot * tt + r], 1)], sem)

    def issue(r, carry):
        row_copy(r, 0).start()
        row_copy(r, 1).start()
        return carry

    def drain(r, carry):
        row_copy(r, 0).wait()
        row_copy(r, 1).wait()
        return carry

    lax.fori_loop(0, tt, issue, 0)
    lax.fori_loop(0, tt, drain, 0)


def _dispatch(pos, h, xs, tt):
    any_spec = pl.BlockSpec(memory_space=pl.ANY)
    return pl.pallas_call(
        functools.partial(_dispatch_kernel, tt=tt),
        grid=(pos.shape[0],),
        in_specs=[pl.BlockSpec((None, 1, 2 * tt), lambda i: (i, 0, 0), memory_space=pltpu.SMEM),
                  any_spec, any_spec],
        out_specs=any_spec,
        out_shape=jax.ShapeDtypeStruct(xs.shape, xs.dtype),
        scratch_shapes=[pltpu.SemaphoreType.DMA(())],
        input_output_aliases={2: 0},
        compiler_params=_params("arbitrary"),
        name="moe_dispatch",
    )(pos, h, xs)


def _combine_kernel(pos_ref, x_ref, r_ref, g_ref, y_hbm, o_ref, ybuf, sem, *, tt, final_norm):
    def row_copy(r, slot):
        return pltpu.make_async_copy(y_hbm.at[pl.ds(pos_ref[0, slot * tt + r], 1)],
                                     ybuf.at[slot, pl.ds(r, 1)], sem)

    def issue(r, carry):
        row_copy(r, 0).start()
        row_copy(r, 1).start()
        return carry

    def drain(r, carry):
        row_copy(r, 0).wait()
        row_copy(r, 1).wait()
        return carry

    lax.fori_loop(0, tt, issue, 0)
    lax.fori_loop(0, tt, drain, 0)
    r = r_ref[...]
    x = x_ref[...] + (r[:, 2:3] * ybuf[0] + r[:, 3:4] * ybuf[1])
    o_ref[...] = _rms(x, g_ref[...]) if final_norm else x


def _combine(pos, x, route, g, y, final_norm, tt):
    t, d = x.shape
    row = lambda i: (i, 0)
    return pl.pallas_call(
        functools.partial(_combine_kernel, tt=tt, final_norm=final_norm),
        grid=(t // tt,),
        in_specs=[pl.BlockSpec((None, 1, 2 * tt), lambda i: (i, 0, 0), memory_space=pltpu.SMEM),
                  pl.BlockSpec((tt, d), row), pl.BlockSpec((tt, LANES), row),
                  pl.BlockSpec((1, d), lambda i: (0, 0)), pl.BlockSpec(memory_space=pl.ANY)],
        out_specs=pl.BlockSpec((tt, d), row),
        out_shape=jax.ShapeDtypeStruct((t, d), F32),
        scratch_shapes=[pltpu.VMEM((TOP_K, tt, d), F32), pltpu.SemaphoreType.DMA(())],
        compiler_params=_params("arbitrary"),
        name="moe_combine",
    )(pos, x, route, g, y)


def _route_plan(route, ranks, counts, n_experts, tm, n_tiles_max):
    cnt = counts[0, :n_experts].astype(jnp.int32)
    tiles = (cnt + tm - 1) // tm
    tile_end = jnp.cumsum(tiles)
    start = (tile_end - tiles) * tm
    experts = jnp.arange(n_experts, dtype=jnp.int32)
    idx = route[:, :TOP_K].astype(jnp.int32)
    pos = jnp.sum(jnp.where(idx[..., None] == experts, start, 0), axis=-1) + ranks[:, :TOP_K].astype(jnp.int32)
    tile_ids = jnp.arange(n_tiles_max, dtype=jnp.int32)
    tile_expert = jnp.minimum(jnp.sum((tile_end[None, :] <= tile_ids[:, None]).astype(jnp.int32), axis=1),
                              n_experts - 1)
    return pos, tile_expert, tile_end[-1:]


def _pos_tiles(pos, tt):
    n = pos.shape[0] // tt
    return pos.reshape(n, tt, TOP_K).transpose(0, 2, 1).reshape(n, 1, TOP_K * tt)


def _pick_tile(n, target):
    t = min(n, target)
    while n % t:
        t //= 2
    return t


def _ff_tile(f, target):
    best = LANES
    for k in range(1, f // LANES + 1):
        if f % (k * LANES) == 0 and k * LANES <= target:
            best = k * LANES
    return best


def kernel(x_prompt, x_sample, cache_k, cache_v, page_table, w_in, w_out, g_mix_norm, g_sgu_norm, diff_subln_g, lambda_q1, lambda_k1, lambda_q2, lambda_k2, sgu_w, sgu_b, g_ffn_norm, ffn_w_gate, ffn_w_up, ffn_w_down, moe_w_router, moe_w_gate, moe_w_up, moe_w_down, g_final_norm):
    bp, sp, d = x_prompt.shape
    bs, ss, _ = x_sample.shape
    assert ss == 1, "the sample group decodes one token per sequence"
    depth = w_in.shape[0]
    assert depth % 2 == 0, "the last layer must be a routed layer (it applies the final norm)"
    d_sgu = g_sgu_norm.shape[1]
    d_attn = (w_in.shape[2] - 2 * d_sgu) // 3
    n_experts = moe_w_router.shape[2]
    n_pool, page = cache_k.shape[1], cache_k.shape[2]
    tp = bp * sp

    tm_rows = _pick_tile(sp, 512)
    tm_ffn = _pick_tile(tp, 1024)
    tm_moe = _pick_tile(tp, 1024)
    pages = _pick_tile(page_table.shape[1], 8)

    xp = x_prompt.reshape(tp, d)
    xs = x_sample.reshape(bs, d)
    dv = d_attn // N_DIFF_HEADS
    ck = cache_k.reshape(depth, n_pool, page * N_DIFF_HEADS, dv)
    cv = cache_v.reshape(depth, n_pool, page * N_DIFF_HEADS, dv)
    row2 = lambda a: a.reshape(1, -1)
    g_final = row2(g_final_norm)
    zero1 = jnp.zeros((1,), jnp.int32)

    k_p, v_p, k_s, v_s, gv_s = [], [], [], [], []
    for l in range(depth):
        lam_init = 0.8 - 0.6 * math.exp(-0.3 * l)
        lamp = jnp.stack([lambda_q1[l], lambda_k1[l], lambda_q2[l], lambda_k2[l]]).astype(F32)
        w_in_l = w_in[l].astype(BF16)
        w_out_l = w_out[l].astype(BF16)
        g_mix, g_sgu, g_sub, g_ffn = row2(g_mix_norm[l]), row2(g_sgu_norm[l]), row2(diff_subln_g[l]), row2(g_ffn_norm[l])
        moe = l % 2 == 1
        i = l // 2
        if moe:
            wr = jnp.zeros((d, LANES), F32).at[:, :n_experts].set(moe_w_router[i])
            wr_hi = wr.astype(BF16)
            wr_f32 = jnp.stack([wr, jnp.zeros_like(wr)])
            wr = jnp.stack([wr_hi, (wr - wr_hi.astype(F32)).astype(BF16)])
            ne = n_experts
        else:
            wr, wr_f32, ne = None, None, 0

        qt, kf, kb, vf, vt, s = _in_proj_prompt(xp, g_mix, w_in_l, g_sgu, sgu_w[l], sgu_b[l].T, tm_rows)
        a = _attn_prompt(lamp, g_sub.reshape(dv, 1), qt, kb, vt, bp, lam_init)
        outs_p = _out_proj(a, s, xp, w_out_l, g_ffn, wr, ne, tm_rows)
        k_p.append(kf)
        v_p.append(vf)

        ge = d_sgu // N_SGU_GROUPS
        wd = row2(jnp.repeat(sgu_w[l][:, 0, 0], ge))
        bd = row2(jnp.repeat(sgu_b[l][:, 0], ge))
        q2, k2, v2, s2, gv2 = _in_proj_single(xs, g_mix, w_in[l], g_sgu, wd, bd)
        a2 = _attn_decode(page_table, lamp, g_sub, q2, k2, v2, ck, cv, l, pages, lam_init)
        outs_s = _out_proj(a2, s2, xs, w_out[l], g_ffn, wr_f32, ne, bs)
        k_s.append(k2)
        v_s.append(v2)
        gv_s.append(gv2)

        if not moe:
            wg = ffn_w_gate[i].astype(BF16)[None]
            wu = ffn_w_up[i].astype(BF16)[None]
            wdn = ffn_w_down[i].astype(BF16)[None]
            tf = _ff_tile(wg.shape[2], 1408)
            (xp_mid, hp), (xs_mid, hs) = outs_p, outs_s
            xp = _ffn(jnp.zeros((tp // tm_ffn,), jnp.int32), jnp.full((1,), tp // tm_ffn, jnp.int32),
                      hp, wg, wu, wdn, xp_mid, tm_ffn, tf)
            xs = _ffn(zero1, zero1 + 1, hs, ffn_w_gate[i][None], ffn_w_up[i][None], ffn_w_down[i][None],
                      xs_mid, bs, _ff_tile(wg.shape[2], 256))
        else:
            (xp_mid, hp, rp), (xs_mid, hs, rs) = outs_p, outs_s
            wg = moe_w_gate[i].astype(BF16)
            wu = moe_w_up[i].astype(BF16)
            wdn = moe_w_down[i].astype(BF16)
            tf = _ff_tile(wg.shape[2], 896)
            route = jnp.concatenate([rp, jnp.pad(rs, ((0, tm_rows - bs), (0, 0)), constant_values=-1.0)], axis=0)
            ranks, counts = _rank(route, tm_rows)
            n_tiles_max = (TOP_K * (tp + bs)) // tm_moe + n_experts
            pos, tile_expert, n_tiles = _route_plan(route, ranks, counts, n_experts, tm_moe, n_tiles_max)
            pos_p = _pos_tiles(pos[:tp], tm_rows)
            pos_s = _pos_tiles(pos[tp:tp + bs], bs)
            x_sorted = jnp.zeros((n_tiles_max * tm_moe, d), F32)
            x_sorted = _dispatch(pos_p, hp, x_sorted, tm_rows)
            x_sorted = _dispatch(pos_s, hs, x_sorted, bs)
            y_sorted = _ffn(tile_expert, n_tiles, x_sorted, wg, wu, wdn, None, tm_moe, tf)
            final = l == depth - 1
            xp = _combine(pos_p, xp_mid, rp, g_final, y_sorted, final, tm_rows)
            xs = _combine(pos_s, xs_mid, rs, g_final, y_sorted, final, bs)

    nh = N_DIFF_HEADS
    return (xp.reshape(bp, sp, d), xs.reshape(bs, ss, d),
            jnp.stack(k_p).reshape(depth, bp, sp, nh, dv), jnp.stack(v_p).reshape(depth, bp, sp, nh, dv),
            jnp.stack(k_s).reshape(depth, bs, ss, nh, dv), jnp.stack(v_s).reshape(depth, bs, ss, nh, dv),
            jnp.stack(gv_s).reshape(depth, bs, ss, d_sgu))
```

```python
import functools
import math

import jax
import jax.numpy as jnp
import numpy as np
from jax import lax
from jax.experimental import pallas as pl
from jax.experimental.pallas import tpu as pltpu

F32 = jnp.float32
BF16 = jnp.bfloat16
EPS = 1e-6

N_DIFF_HEADS = 4
N_SGU_GROUPS = 4
CHUNK = 128
TOP_K = 2
LANES = 128
VMEM_LIMIT = 56 * 1024 * 1024
LOG2E = math.log2(math.e)
ALIBI_TERMS = 4
DMA_UNROLL = 8

NT_DIMS = (((1,), (1,)), ((), ()))


def _rms(x, g):
    return x * lax.rsqrt(jnp.mean(x * x, axis=-1, keepdims=True) + EPS) * g


def _mm(a, w):
    return jnp.dot(a.astype(BF16), w, preferred_element_type=F32)


def _params(*sem):
    return pltpu.CompilerParams(dimension_semantics=sem, vmem_limit_bytes=VMEM_LIMIT)


def _in_proj_common(x_ref, g_ref, w_ref, gsgu_ref, d_attn, d_sgu, q_scale=1.0):
    h = _rms(x_ref[...], g_ref[...])
    if w_ref.dtype == BF16:
        h = h.astype(BF16)

    def proj(c0, c1):
        return _mm(h, w_ref[:, c0:c1])

    q = proj(0, d_attn) * (q_scale * float(d_attn // N_DIFF_HEADS // 2) ** -0.5)
    k = proj(d_attn, 2 * d_attn)
    v = proj(2 * d_attn, 3 * d_attn)
    u = jax.nn.gelu(proj(3 * d_attn, 3 * d_attn + d_sgu))
    gv = _rms(jax.nn.gelu(proj(3 * d_attn + d_sgu, 3 * d_attn + 2 * d_sgu)), gsgu_ref[...])
    return q, k, v, u, gv


def _in_proj_prompt_kernel(x_ref, g_ref, w_ref, gsgu_ref, sw_ref, sb_ref,
                           q_ref, kf_ref, kb_ref, vf_ref, vb_ref, s_ref, *, d_attn, d_sgu):
    q, k, v, u, gv = _in_proj_common(x_ref, g_ref, w_ref, gsgu_ref, d_attn, d_sgu, LOG2E)
    q_ref[...] = q.T.astype(BF16)
    kf_ref[...] = k
    kb_ref[...] = k.astype(BF16)
    vf_ref[...] = v
    vb_ref[...] = v.T.astype(BF16)
    tm = x_ref.shape[0]
    ge = d_sgu // N_SGU_GROUPS
    row = lax.broadcasted_iota(jnp.int32, (CHUNK, CHUNK), 0)
    col = lax.broadcasted_iota(jnp.int32, (CHUNK, CHUNK), 1)
    gvb = gv.astype(BF16)
    for g in range(N_SGU_GROUPS):
        wt = jnp.where(col <= row, sw_ref[g], 0.0).astype(BF16)
        bg = sb_ref[:, g:g + 1]
        for c in range(tm // CHUNK):
            r0, c0 = c * CHUNK, g * ge
            mixed = jnp.dot(wt, gvb[r0:r0 + CHUNK, c0:c0 + ge], preferred_element_type=F32) + bg
            s_ref[r0:r0 + CHUNK, c0:c0 + ge] = (u[r0:r0 + CHUNK, c0:c0 + ge] * mixed).astype(BF16)


def _in_proj_single_kernel(x_ref, g_ref, w_ref, gsgu_ref, wd_ref, bd_ref,
                           q_ref, k_ref, v_ref, s_ref, gv_ref, *, d_attn, d_sgu):
    q, k, v, u, gv = _in_proj_common(x_ref, g_ref, w_ref, gsgu_ref, d_attn, d_sgu)
    q_ref[...] = q
    k_ref[...] = k
    v_ref[...] = v
    gv_ref[...] = gv
    s_ref[...] = u * (gv * wd_ref[...] + bd_ref[...])


def _in_proj_prompt(x, g, w, gsgu, sw, sb_t, tm):
    t, d = x.shape
    d_sgu = gsgu.shape[1]
    d_attn = (w.shape[1] - 2 * d_sgu) // 3
    row = lambda i: (i, 0)
    full2 = lambda i: (0, 0)
    o_f32 = jax.ShapeDtypeStruct((t, d_attn), F32)
    o_bf = jax.ShapeDtypeStruct((t, d_attn), BF16)
    o_t = jax.ShapeDtypeStruct((t // tm, d_attn, tm), BF16)
    spec = pl.BlockSpec((tm, d_attn), row)
    spec_t = pl.BlockSpec((None, d_attn, tm), lambda i: (i, 0, 0))
    return pl.pallas_call(
        functools.partial(_in_proj_prompt_kernel, d_attn=d_attn, d_sgu=d_sgu),
        grid=(t // tm,),
        in_specs=[pl.BlockSpec((tm, d), row), pl.BlockSpec((1, d), full2),
                  pl.BlockSpec(w.shape, full2), pl.BlockSpec((1, d_sgu), full2),
                  pl.BlockSpec(sw.shape, lambda i: (0, 0, 0)), pl.BlockSpec(sb_t.shape, full2)],
        out_specs=[spec_t, spec, spec, spec, spec_t, pl.BlockSpec((tm, d_sgu), row)],
        out_shape=[o_t, o_f32, o_bf, o_f32, o_t, jax.ShapeDtypeStruct((t, d_sgu), BF16)],
        compiler_params=_params("parallel"),
        name="in_proj_prompt",
    )(x, g, w, gsgu, sw, sb_t)


def _in_proj_single(x, g, w, gsgu, wd, bd):
    t, d = x.shape
    d_sgu = gsgu.shape[1]
    d_attn = (w.shape[1] - 2 * d_sgu) // 3
    full2 = lambda i: (0, 0)
    o_a = jax.ShapeDtypeStruct((t, d_attn), F32)
    o_s = jax.ShapeDtypeStruct((t, d_sgu), F32)
    return pl.pallas_call(
        functools.partial(_in_proj_single_kernel, d_attn=d_attn, d_sgu=d_sgu),
        grid=(1,),
        in_specs=[pl.BlockSpec(x.shape, full2), pl.BlockSpec((1, d), full2),
                  pl.BlockSpec(w.shape, full2), pl.BlockSpec((1, d_sgu), full2),
                  pl.BlockSpec((1, d_sgu), full2), pl.BlockSpec((1, d_sgu), full2)],
        out_specs=[pl.BlockSpec((t, d_attn), full2)] * 3 + [pl.BlockSpec((t, d_sgu), full2)] * 2,
        out_shape=[o_a, o_a, o_a, o_s, o_s],
        compiler_params=_params("arbitrary"),
        name="in_proj_single",
    )(x, g, w, gsgu, wd, bd)


def _lambda(lamp_ref, lam_init):
    lp = lamp_ref[...]
    return (jnp.exp(jnp.sum(lp[0:1] * lp[1:2], keepdims=True))
            - jnp.exp(jnp.sum(lp[2:3] * lp[3:4], keepdims=True)) + lam_init)


def _head_slope(h):
    slope = jnp.float32(0.0)
    for i in range(N_DIFF_HEADS):
        slope = jnp.where(h == i, jnp.float32(2.0 ** (-8.0 * (i + 1) / N_DIFF_HEADS)), slope)
    return slope


def _alibi_columns(n_heads):
    cols = np.zeros((n_heads, 2 * ALIBI_TERMS, 1), np.float32)
    for h in range(n_heads):
        rest = 2.0 ** (-8.0 * (h + 1) / n_heads) * LOG2E
        for t in range(ALIBI_TERMS):
            term = float(np.asarray(rest, np.float32).astype(BF16))
            cols[h, t, 0] = 16.0 * term
            cols[h, ALIBI_TERMS + t, 0] = term
            rest -= term
    return jnp.asarray(cols)


def _attn_prompt_kernel(lamp_ref, g_ref, ab_ref, qt_ref, k_ref, vt_ref, o_ref, m_ref, l_ref, acc_ref,
                        *, tq, lam_init):
    h = pl.program_id(1)
    qi = pl.program_id(2)
    slope = _head_slope(h) * LOG2E
    dv = qt_ref.shape[0]
    qt = qt_ref[...]
    feat = lax.broadcasted_iota(jnp.int32, qt.shape, 0)
    zero = jnp.zeros_like(qt)
    q0 = qi * tq

    nf = ab_ref.shape[0]
    frow = lax.broadcasted_iota(jnp.int32, (dv, tq), 0)
    q_extra = jnp.zeros((dv, tq), F32)
    for t in range(nf):
        q_extra = jnp.where(frow == t, jnp.broadcast_to(ab_ref[t:t + 1, :], (dv, tq)), q_extra)
    q_extra = q_extra.astype(BF16)
    qts = (jnp.concatenate([jnp.where(feat < dv // 2, qt, zero), q_extra], axis=0),
           jnp.concatenate([jnp.where(feat >= dv // 2, qt, zero), q_extra], axis=0))
    j = lax.broadcasted_iota(jnp.int32, (tq, dv), 0)
    fcol = lax.broadcasted_iota(jnp.int32, (tq, dv), 1)
    k_extra = jnp.where(fcol < nf // 2, j // 16, jnp.where(fcol < nf, j % 16, 0)).astype(F32).astype(BF16)

    m_ref[...] = jnp.full(m_ref.shape, -jnp.inf, F32)
    l_ref[...] = jnp.zeros(l_ref.shape, F32)
    acc_ref[...] = jnp.zeros(acc_ref.shape, F32)

    def block(kb, masked):
        k0 = pl.multiple_of(kb * tq, tq)
        k = jnp.concatenate([k_ref[pl.ds(k0, tq), :], k_extra], axis=1)
        vt = vt_ref[kb]
        shift = jnp.full((1, tq), k0 - q0, jnp.int32).astype(F32) * slope
        if masked:
            keep = (lax.broadcasted_iota(jnp.int32, (tq, tq), 0)
                    <= lax.broadcasted_iota(jnp.int32, (tq, tq), 1))
        for c in range(2):
            s = jnp.dot(k, qts[c], preferred_element_type=F32)
            if masked:
                s = jnp.where(keep, s, -jnp.inf)
            m_old = m_ref[c]
            m_new = jnp.maximum(m_old, jnp.max(s, axis=0, keepdims=True) + shift)
            p = jnp.exp2(s - (m_new - shift))
            alpha = jnp.exp2(m_old - m_new)
            l_ref[c] = alpha * l_ref[c] + jnp.sum(p, axis=0, keepdims=True)
            acc_ref[c] = alpha * acc_ref[c] + jnp.dot(vt, p.astype(BF16), preferred_element_type=F32)
            m_ref[c] = m_new

    def body(kb, carry):
        block(kb, False)
        return carry

    lax.fori_loop(0, qi, body, 0)
    block(qi, True)

    lam = _lambda(lamp_ref, lam_init)
    ot = acc_ref[0] * (1.0 / l_ref[0]) - lam * (acc_ref[1] * (1.0 / l_ref[1]))
    ot = ot * lax.rsqrt(jnp.mean(ot * ot, axis=0, keepdims=True) + EPS) * g_ref[...] * (1.0 - lam_init)
    o_ref[...] = ot.T.astype(BF16)


def _attn_prompt(lamp, g_col, qt, k, vt, batch, lam_init):
    nblk, d_attn, tq = qt.shape
    t = nblk * tq
    s = t // batch
    dv = d_attn // N_DIFF_HEADS
    nq = s // tq
    alibi = _alibi_columns(N_DIFF_HEADS)
    return pl.pallas_call(
        functools.partial(_attn_prompt_kernel, tq=tq, lam_init=lam_init),
        grid=(batch, N_DIFF_HEADS, nq),
        in_specs=[pl.BlockSpec(lamp.shape, lambda b, h, i: (0, 0)),
                  pl.BlockSpec((dv, 1), lambda b, h, i: (0, 0)),
                  pl.BlockSpec((None,) + alibi.shape[1:], lambda b, h, i: (h, 0, 0)),
                  pl.BlockSpec((None, dv, tq), lambda b, h, i: (b * nq + i, h, 0)),
                  pl.BlockSpec((s, dv), lambda b, h, i: (b, h)),
                  pl.BlockSpec((nq, dv, tq), lambda b, h, i: (b, h, 0))],
        out_specs=pl.BlockSpec((tq, dv), lambda b, h, i: (b * nq + i, h)),
        out_shape=jax.ShapeDtypeStruct((t, d_attn), BF16),
        scratch_shapes=[pltpu.VMEM((2, 1, tq), F32), pltpu.VMEM((2, 1, tq), F32),
                        pltpu.VMEM((2, dv, tq), F32)],
        compiler_params=_params("parallel", "parallel", "arbitrary"),
        name="attn_prompt",
    )(lamp, g_col, alibi, qt, k, vt)


def _attn_decode_kernel(pt_ref, lamp_ref, g_ref, q_ref, kn_ref, vn_ref, *rest,
                        pages, n_past, lam_init):
    del pt_ref
    k_refs, v_refs = rest[:pages], rest[pages:2 * pages]
    o_ref, m_ref, l_ref, acc_ref = rest[2 * pages:]
    nh = N_DIFF_HEADS
    j = pl.program_id(1)
    page = k_refs[0].shape[0] // nh
    dv = q_ref.shape[1] // nh
    span = pages * page
    rows = 2 * nh

    @pl.when(j == 0)
    def _():
        m_ref[...] = jnp.full(m_ref.shape, -jnp.inf, F32)
        l_ref[...] = jnp.zeros(l_ref.shape, F32)
        acc_ref[...] = jnp.zeros(acc_ref.shape, F32)

    def head_rows(shape):
        return lax.broadcasted_iota(jnp.int32, shape, 0) // 2

    def stack_heads(ref):
        hr = head_rows((rows, dv))
        out = jnp.zeros((rows, dv), F32)
        for h in range(nh):
            out = jnp.where(hr == h, jnp.broadcast_to(ref[:, h * dv:(h + 1) * dv], (rows, dv)), out)
        return out

    rowi = lax.broadcasted_iota(jnp.int32, (rows, dv), 0)
    lane = lax.broadcasted_iota(jnp.int32, (rows, dv), 1)
    q8 = jnp.where((lane >= dv // 2) == (rowi % 2 == 1), stack_heads(q_ref), 0.0)
    q8b = q8.astype(BF16)
    slope = jnp.zeros((rows, 1), F32)
    for h in range(nh):
        slope = jnp.where(head_rows((rows, 1)) == h, jnp.float32(2.0 ** (-8.0 * (h + 1) / nh)), slope)

    def update(s, pv_fn):
        m_old = m_ref[...]
        m_new = jnp.maximum(m_old, jnp.max(s, axis=-1, keepdims=True))
        p = jnp.exp(s - m_new)
        alpha = jnp.exp(m_old - m_new)
        l_ref[...] = alpha * l_ref[...] + jnp.sum(p, axis=-1, keepdims=True)
        acc_ref[...] = alpha * acc_ref[...] + pv_fn(p)
        m_ref[...] = m_new

    def head_page(ref, h):
        return ref[pl.ds(h, page, stride=nh), :].astype(BF16)

    hr_s = head_rows((rows, span))
    s = None
    for h in range(nh):
        sh = jnp.concatenate(
            [lax.dot_general(q8b, head_page(k_refs[i], h), NT_DIMS, preferred_element_type=F32)
             for i in range(pages)], axis=-1)
        s = sh if s is None else jnp.where(hr_s == h, sh, s)
    kpos = j * span + lax.broadcasted_iota(jnp.int32, (1, span), 1)
    s = s - slope * (n_past - kpos).astype(F32)

    def pv(p):
        pb = p.astype(BF16)
        hr_v = head_rows((rows, dv))
        out = None
        for h in range(nh):
            oh = jnp.zeros((rows, dv), F32)
            for i in range(pages):
                oh = oh + jnp.dot(pb[:, i * page:(i + 1) * page], head_page(v_refs[i], h),
                                  preferred_element_type=F32)
            out = oh if out is None else jnp.where(hr_v == h, oh, out)
        return out

    update(s, pv)

    @pl.when(j == pl.num_programs(1) - 1)
    def _():
        s_new = jnp.sum(q8 * stack_heads(kn_ref), axis=-1, keepdims=True)
        vn8 = stack_heads(vn_ref)
        update(s_new, lambda p: p * vn8)
        lam = _lambda(lamp_ref, lam_init)
        a = acc_ref[...] * (1.0 / l_ref[...])
        for h in range(nh):
            o = a[2 * h:2 * h + 1] - lam * a[2 * h + 1:2 * h + 2]
            o_ref[:, h * dv:(h + 1) * dv] = _rms(o, g_ref[...]) * (1.0 - lam_init)


def _attn_decode(page_table, lamp, g, q, k_new, v_new, cache_k, cache_v, layer, pages, lam_init):
    bs, d_attn = q.shape
    n_pages = page_table.shape[1]
    rows_per_page, dv = cache_k.shape[2], cache_k.shape[3]
    page = rows_per_page // N_DIFF_HEADS
    steps = n_pages // pages
    row3 = lambda b, j, pt: (b, 0, 0)

    def page_spec(i):
        return pl.BlockSpec((None, None, rows_per_page, dv),
                            lambda b, j, pt, i=i: (layer, pt[b, j * pages + i], 0, 0))

    grid_spec = pltpu.PrefetchScalarGridSpec(
        num_scalar_prefetch=1,
        grid=(bs, steps),
        in_specs=[pl.BlockSpec(lamp.shape, lambda b, j, pt: (0, 0)),
                  pl.BlockSpec((1, dv), lambda b, j, pt: (0, 0)),
                  pl.BlockSpec((None, 1, d_attn), row3),
                  pl.BlockSpec((None, 1, d_attn), row3),
                  pl.BlockSpec((None, 1, d_attn), row3)]
                 + [page_spec(i) for i in range(pages)] * 2,
        out_specs=pl.BlockSpec((None, 1, d_attn), row3),
        scratch_shapes=[pltpu.VMEM((2 * N_DIFF_HEADS, 1), F32), pltpu.VMEM((2 * N_DIFF_HEADS, 1), F32),
                        pltpu.VMEM((2 * N_DIFF_HEADS, dv), F32)],
    )
    r3 = lambda a: a.reshape(bs, 1, d_attn)
    out = pl.pallas_call(
        functools.partial(_attn_decode_kernel, pages=pages, n_past=n_pages * page, lam_init=lam_init),
        grid_spec=grid_spec,
        out_shape=jax.ShapeDtypeStruct((bs, 1, d_attn), F32),
        compiler_params=_params("parallel", "arbitrary"),
        name="attn_decode",
    )(page_table, lamp, g, r3(q), r3(k_new), r3(v_new), *([cache_k] * pages), *([cache_v] * pages))
    return out.reshape(bs, d_attn)


def _top2_route(logits, n_experts):
    lane = lax.broadcasted_iota(jnp.int32, logits.shape, 1)
    big = jnp.int32(logits.shape[1])
    lg = jnp.where(lane < n_experts, logits, -jnp.inf)
    m1 = jnp.max(lg, axis=-1, keepdims=True)
    i1 = jnp.min(jnp.where(lg == m1, lane, big), axis=-1, keepdims=True)
    lg2 = jnp.where(lane == i1, -jnp.inf, lg)
    m2 = jnp.max(lg2, axis=-1, keepdims=True)
    i2 = jnp.min(jnp.where(lg2 == m2, lane, big), axis=-1, keepdims=True)
    e2 = jnp.exp(m2 - m1)
    den = 1.0 + e2
    packed = jnp.where(lane == 0, i1.astype(F32), 0.0)
    packed = jnp.where(lane == 1, i2.astype(F32), packed)
    packed = jnp.where(lane == 2, 1.0 / den, packed)
    packed = jnp.where(lane == 3, e2 / den, packed)
    return packed


def _out_proj_kernel(*refs, n_experts):
    if n_experts:
        a_ref, s_ref, x_ref, w_ref, g_ref, wr_ref, xo_ref, h_ref, r_ref = refs
    else:
        a_ref, s_ref, x_ref, w_ref, g_ref, xo_ref, h_ref = refs
    d_attn = a_ref.shape[1]
    x = x_ref[...] + _mm(a_ref[...], w_ref[:d_attn, :]) + _mm(s_ref[...], w_ref[d_attn:, :])
    xo_ref[...] = x
    h = _rms(x, g_ref[...])
    h_ref[...] = h.astype(h_ref.dtype)
    if n_experts:
        if wr_ref.dtype == F32:
            logits = jnp.dot(h, wr_ref[0], preferred_element_type=F32, precision=lax.Precision.HIGHEST)
        else:
            h_hi = h.astype(BF16)
            h_lo = (h - h_hi.astype(F32)).astype(BF16)
            logits = (jnp.dot(h_hi, wr_ref[0], preferred_element_type=F32)
                      + (jnp.dot(h_lo, wr_ref[0], preferred_element_type=F32)
                         + jnp.dot(h_hi, wr_ref[1], preferred_element_type=F32)))
        r_ref[...] = _top2_route(logits, n_experts)


def _out_proj(a, s, x, w, g, wr, n_experts, tm):
    t, d = x.shape
    d_attn, d_sgu = a.shape[1], s.shape[1]
    row = lambda i: (i, 0)
    full2 = lambda i: (0, 0)
    in_specs = [pl.BlockSpec((tm, d_attn), row), pl.BlockSpec((tm, d_sgu), row),
                pl.BlockSpec((tm, d), row), pl.BlockSpec(w.shape, full2), pl.BlockSpec((1, d), full2)]
    out_specs = [pl.BlockSpec((tm, d), row), pl.BlockSpec((tm, d), row)]
    out_shape = [jax.ShapeDtypeStruct((t, d), F32), jax.ShapeDtypeStruct((t, d), BF16 if (w.dtype == BF16 and not n_experts) else F32)]
    args = [a, s, x, w, g]
    if n_experts:
        in_specs.append(pl.BlockSpec(wr.shape, lambda i: (0, 0, 0)))
        out_specs.append(pl.BlockSpec((tm, LANES), row))
        out_shape.append(jax.ShapeDtypeStruct((t, LANES), F32))
        args.append(wr)
    return pl.pallas_call(
        functools.partial(_out_proj_kernel, n_experts=n_experts),
        grid=(t // tm,),
        in_specs=in_specs, out_specs=out_specs, out_shape=out_shape,
        compiler_params=_params("parallel"),
        name="out_proj",
    )(*args)


def _ffn_kernel(te_ref, nt_ref, x_ref, wg_ref, wu_ref, wd_ref, *rest, residual, cast_x):
    del te_ref
    rest = list(rest)
    res_ref = rest.pop(0) if residual else None
    o_ref, acc_ref = rest[0], rest[1]
    xb_ref = rest[2] if cast_x else x_ref
    i = pl.program_id(0)
    j = pl.program_id(1)

    @pl.when((i >= nt_ref[0]) & (j == pl.num_programs(1) - 1))
    def _():
        o_ref[...] = jnp.zeros(o_ref.shape, F32)

    @pl.when(i < nt_ref[0])
    def _():
        @pl.when(j == 0)
        def _():
            acc_ref[...] = jnp.zeros(acc_ref.shape, F32)
            if cast_x:
                xb_ref[...] = x_ref[...].astype(BF16)

        x = xb_ref[...]
        gt = _mm(x, wg_ref[...])
        up = _mm(x, wu_ref[...])
        acc_ref[...] += _mm(gt * jax.nn.sigmoid(gt) * up, wd_ref[...])

        @pl.when(j == pl.num_programs(1) - 1)
        def _():
            o_ref[...] = res_ref[...] + acc_ref[...] if residual else acc_ref[...]


def _ffn(tile_expert, n_tiles, x, wg, wu, wd, res, tm, tf):
    r, d = x.shape
    f = wg.shape[2]
    residual = res is not None
    cast_x = x.dtype != BF16 and wg.dtype == BF16

    def xi(i, j, te, nt):
        return (jnp.minimum(i, nt[0] - 1), 0)

    def ji(i, j, nt):
        return jnp.where(i < nt[0], j, f // tf - 1)

    in_specs = [pl.BlockSpec((tm, d), xi),
                pl.BlockSpec((None, d, tf), lambda i, j, te, nt: (te[i], 0, ji(i, j, nt))),
                pl.BlockSpec((None, d, tf), lambda i, j, te, nt: (te[i], 0, ji(i, j, nt))),
                pl.BlockSpec((None, tf, d), lambda i, j, te, nt: (te[i], ji(i, j, nt), 0))]
    args = [tile_expert, n_tiles, x, wg, wu, wd]
    if residual:
        in_specs.append(pl.BlockSpec((tm, d), xi))
        args.append(res)
    scratch = [pltpu.VMEM((tm, d), F32)] + ([pltpu.VMEM((tm, d), BF16)] if cast_x else [])
    grid_spec = pltpu.PrefetchScalarGridSpec(
        num_scalar_prefetch=2,
        grid=(r // tm, f // tf),
        in_specs=in_specs,
        out_specs=pl.BlockSpec((tm, d), lambda i, j, te, nt: (i, 0)),
        scratch_shapes=scratch,
    )
    return pl.pallas_call(
        functools.partial(_ffn_kernel, residual=residual, cast_x=cast_x),
        grid_spec=grid_spec,
        out_shape=jax.ShapeDtypeStruct((r, d), F32),
        compiler_params=_params("arbitrary", "arbitrary"),
        name="ffn_residual" if residual else "ffn_routed",
    )(*args)


def _rank_kernel(r_ref, rank_ref, cnt_ref, carry_ref):
    @pl.when(pl.program_id(0) == 0)
    def _():
        carry_ref[...] = jnp.zeros(carry_ref.shape, F32)

    r = r_ref[...]
    tt = r.shape[0]
    lane = lax.broadcasted_iota(jnp.int32, (tt, LANES), 1)
    lane_f = lane.astype(F32)
    oh1 = lane_f == r[:, 0:1]
    oh2 = lane_f == r[:, 1:2]
    chosen = jnp.where(oh1, 1.0, jnp.where(oh2, 1.0, 0.0))
    earlier = jnp.where(lax.broadcasted_iota(jnp.int32, (tt, tt), 1)
                        < lax.broadcasted_iota(jnp.int32, (tt, tt), 0), 1.0, 0.0).astype(BF16)
    before = jnp.dot(earlier, chosen.astype(BF16), preferred_element_type=F32) + carry_ref[...]
    rank1 = jnp.sum(jnp.where(oh1, before, 0.0), axis=-1, keepdims=True)
    rank2 = jnp.sum(jnp.where(oh2, before, 0.0), axis=-1, keepdims=True)
    rank_ref[...] = jnp.where(lane == 0, rank1, jnp.where(lane == 1, rank2, 0.0))
    carry_ref[...] += jnp.sum(chosen, axis=0, keepdims=True)
    cnt_ref[...] = carry_ref[...]


def _rank(route, tt):
    t = route.shape[0]
    return pl.pallas_call(
        _rank_kernel,
        grid=(t // tt,),
        in_specs=[pl.BlockSpec((tt, LANES), lambda i: (i, 0))],
        out_specs=[pl.BlockSpec((tt, LANES), lambda i: (i, 0)), pl.BlockSpec((1, LANES), lambda i: (0, 0))],
        out_shape=[jax.ShapeDtypeStruct((t, LANES), F32), jax.ShapeDtypeStruct((1, LANES), F32)],
        scratch_shapes=[pltpu.VMEM((1, LANES), F32)],
        compiler_params=_params("arbitrary"),
        name="route_rank",
    )(route)


def _dispatch_kernel(pos_ref, h_ref, xs_in_hbm, xs_hbm, sem, *, tt):
    del xs_in_hbm

    def row_copy(r, slot):
        return pltpu.make_async_copy(h_ref.at[pl.ds(r, 1)],
                                     xs_hbm.at[pl.ds(pos_ref[0, slot * tt + r], 1)], sem)

    def issue(r, carry):
        row_copy(r, 0).start(priority=0)
        row_copy(r, 1).start(priority=1)
        return carry

    def drain(r, carry):
        row_copy(r, 0).wait()
        row_copy(r, 1).wait()
        return carry

    lax.fori_loop(0, tt, issue, 0, unroll=DMA_UNROLL)
    lax.fori_loop(0, tt, drain, 0, unroll=DMA_UNROLL)


def _dispatch(pos, h, xs, tt):
    any_spec = pl.BlockSpec(memory_space=pl.ANY)
    return pl.pallas_call(
        functools.partial(_dispatch_kernel, tt=tt),
        grid=(pos.shape[0],),
        in_specs=[pl.BlockSpec((None, 1, 2 * tt), lambda i: (i, 0, 0), memory_space=pltpu.SMEM),
                  pl.BlockSpec((tt, h.shape[1]), lambda i: (i, 0)), any_spec],
        out_specs=any_spec,
        out_shape=jax.ShapeDtypeStruct(xs.shape, xs.dtype),
        scratch_shapes=[pltpu.SemaphoreType.DMA(())],
        input_output_aliases={2: 0},
        compiler_params=_params("arbitrary"),
        name="moe_dispatch",
    )(pos, h, xs)


def _combine_kernel(pos_ref, x_ref, r_ref, g_ref, y_hbm, o_ref, ybuf, sem, *, tt, final_norm):
    def row_copy(r, slot):
        return pltpu.make_async_copy(y_hbm.at[pl.ds(pos_ref[0, slot * tt + r], 1)],
                                     ybuf.at[slot, pl.ds(r, 1)], sem)

    def issue(r, carry):
        row_copy(r, 0).start(priority=0)
        row_copy(r, 1).start(priority=1)
        return carry

    def drain(r, carry):
        row_copy(r, 0).wait()
        row_copy(r, 1).wait()
        return carry

    lax.fori_loop(0, tt, issue, 0, unroll=DMA_UNROLL)
    lax.fori_loop(0, tt, drain, 0, unroll=DMA_UNROLL)
    r = r_ref[...]
    x = x_ref[...] + (r[:, 2:3] * ybuf[0] + r[:, 3:4] * ybuf[1])
    o_ref[...] = _rms(x, g_ref[...]) if final_norm else x


def _combine(pos, x, route, g, y, final_norm, tt):
    t, d = x.shape
    row = lambda i: (i, 0)
    return pl.pallas_call(
        functools.partial(_combine_kernel, tt=tt, final_norm=final_norm),
        grid=(t // tt,),
        in_specs=[pl.BlockSpec((None, 1, 2 * tt), lambda i: (i, 0, 0), memory_space=pltpu.SMEM),
                  pl.BlockSpec((tt, d), row), pl.BlockSpec((tt, LANES), row),
                  pl.BlockSpec((1, d), lambda i: (0, 0)), pl.BlockSpec(memory_space=pl.ANY)],
        out_specs=pl.BlockSpec((tt, d), row),
        out_shape=jax.ShapeDtypeStruct((t, d), F32),
        scratch_shapes=[pltpu.VMEM((TOP_K, tt, d), F32), pltpu.SemaphoreType.DMA(())],
        compiler_params=_params("arbitrary"),
        name="moe_combine",
    )(pos, x, route, g, y)


def _route_plan(route, ranks, counts, n_experts, tm, n_tiles_max):
    cnt = counts[0, :n_experts].astype(jnp.int32)
    tiles = (cnt + tm - 1) // tm
    tile_end = jnp.cumsum(tiles)
    start = (tile_end - tiles) * tm
    experts = jnp.arange(n_experts, dtype=jnp.int32)
    idx = route[:, :TOP_K].astype(jnp.int32)
    pos = jnp.sum(jnp.where(idx[..., None] == experts, start, 0), axis=-1) + ranks[:, :TOP_K].astype(jnp.int32)
    tile_ids = jnp.arange(n_tiles_max, dtype=jnp.int32)
    tile_expert = jnp.minimum(jnp.sum((tile_end[None, :] <= tile_ids[:, None]).astype(jnp.int32), axis=1),
                              n_experts - 1)
    return pos, tile_expert, tile_end[-1:]


def _pos_tiles(pos, tt):
    n = pos.shape[0] // tt
    return pos.reshape(n, tt, TOP_K).transpose(0, 2, 1).reshape(n, 1, TOP_K * tt)


def _pick_tile(n, target):
    t = min(n, target)
    while n % t:
        t //= 2
    return t


def _ff_tile(f, target):
    best = LANES
    for k in range(1, f // LANES + 1):
        if f % (k * LANES) == 0 and k * LANES <= target:
            best = k * LANES
    return best


def kernel(x_prompt, x_sample, cache_k, cache_v, page_table, w_in, w_out, g_mix_norm, g_sgu_norm, diff_subln_g, lambda_q1, lambda_k1, lambda_q2, lambda_k2, sgu_w, sgu_b, g_ffn_norm, ffn_w_gate, ffn_w_up, ffn_w_down, moe_w_router, moe_w_gate, moe_w_up, moe_w_down, g_final_norm):
    bp, sp, d = x_prompt.shape
    bs, ss, _ = x_sample.shape
    assert ss == 1, "the sample group decodes one token per sequence"
    depth = w_in.shape[0]
    assert depth % 2 == 0, "the last layer must be a routed layer (it applies the final norm)"
    d_sgu = g_sgu_norm.shape[1]
    d_attn = (w_in.shape[2] - 2 * d_sgu) // 3
    n_experts = moe_w_router.shape[2]
    n_pool, page = cache_k.shape[1], cache_k.shape[2]
    tp = bp * sp

    tm_rows = _pick_tile(sp, 512)
    tm_ffn = _pick_tile(tp, 1024)
    tm_moe = _pick_tile(tp, 1024)
    pages = _pick_tile(page_table.shape[1], 8)

    xp = x_prompt.reshape(tp, d)
    xs = x_sample.reshape(bs, d)
    dv = d_attn // N_DIFF_HEADS
    ck = cache_k.reshape(depth, n_pool, page * N_DIFF_HEADS, dv)
    cv = cache_v.reshape(depth, n_pool, page * N_DIFF_HEADS, dv)
    row2 = lambda a: a.reshape(1, -1)
    g_final = row2(g_final_norm)
    zero1 = jnp.zeros((1,), jnp.int32)

    k_p, v_p, k_s, v_s, gv_s = [], [], [], [], []
    for l in range(depth):
        lam_init = 0.8 - 0.6 * math.exp(-0.3 * l)
        lamp = jnp.stack([lambda_q1[l], lambda_k1[l], lambda_q2[l], lambda_k2[l]]).astype(F32)
        w_in_l = w_in[l].astype(BF16)
        w_out_l = w_out[l].astype(BF16)
        g_mix, g_sgu, g_sub, g_ffn = row2(g_mix_norm[l]), row2(g_sgu_norm[l]), row2(diff_subln_g[l]), row2(g_ffn_norm[l])
        moe = l % 2 == 1
        i = l // 2
        if moe:
            wr = jnp.zeros((d, LANES), F32).at[:, :n_experts].set(moe_w_router[i])
            wr_hi = wr.astype(BF16)
            wr_f32 = wr[None]
            wr = jnp.stack([wr_hi, (wr - wr_hi.astype(F32)).astype(BF16)])
            ne = n_experts
        else:
            wr, wr_f32, ne = None, None, 0

        qt, kf, kb, vf, vt, s = _in_proj_prompt(xp, g_mix, w_in_l, g_sgu, sgu_w[l], sgu_b[l].T, tm_rows)
        a = _attn_prompt(lamp, g_sub.reshape(dv, 1), qt, kb, vt, bp, lam_init)
        outs_p = _out_proj(a, s, xp, w_out_l, g_ffn, wr, ne, tm_rows)
        k_p.append(kf)
        v_p.append(vf)

        ge = d_sgu // N_SGU_GROUPS
        wd = row2(jnp.repeat(sgu_w[l][:, 0, 0], ge))
        bd = row2(jnp.repeat(sgu_b[l][:, 0], ge))
        q2, k2, v2, s2, gv2 = _in_proj_single(xs, g_mix, w_in_l, g_sgu, wd, bd)
        a2 = _attn_decode(page_table, lamp, g_sub, q2, k2, v2, ck, cv, l, pages, lam_init)
        outs_s = _out_proj(a2, s2, xs, w_out_l, g_ffn, wr_f32, ne, bs)
        k_s.append(k2)
        v_s.append(v2)
        gv_s.append(gv2)

        if not moe:
            wg = ffn_w_gate[i].astype(BF16)[None]
            wu = ffn_w_up[i].astype(BF16)[None]
            wdn = ffn_w_down[i].astype(BF16)[None]
            tf = _ff_tile(wg.shape[2], 1408)
            (xp_mid, hp), (xs_mid, hs) = outs_p, outs_s
            xp = _ffn(jnp.zeros((tp // tm_ffn,), jnp.int32), jnp.full((1,), tp // tm_ffn, jnp.int32),
                      hp, wg, wu, wdn, xp_mid, tm_ffn, tf)
            xs = _ffn(zero1, zero1 + 1, hs, wg, wu, wdn, xs_mid, bs, tf)
        else:
            (xp_mid, hp, rp), (xs_mid, hs, rs) = outs_p, outs_s
            wg = moe_w_gate[i].astype(BF16)
            wu = moe_w_up[i].astype(BF16)
            wdn = moe_w_down[i].astype(BF16)
            tf = _ff_tile(wg.shape[2], 896)
            route = jnp.concatenate([rp, jnp.pad(rs, ((0, tm_rows - bs), (0, 0)), constant_values=-1.0)], axis=0)
            ranks, counts = _rank(route, tm_rows)
            n_tiles_max = (TOP_K * (tp + bs)) // tm_moe + n_experts
            pos, tile_expert, n_tiles = _route_plan(route, ranks, counts, n_experts, tm_moe, n_tiles_max)
            pos_p = _pos_tiles(pos[:tp], tm_rows)
            pos_s = _pos_tiles(pos[tp:tp + bs], bs)
            x_sorted = jnp.zeros((n_tiles_max * tm_moe, d), F32)
            x_sorted = _dispatch(pos_p, hp, x_sorted, tm_rows)
            x_sorted = _dispatch(pos_s, hs, x_sorted, bs)
            y_sorted = _ffn(tile_expert, n_tiles, x_sorted, wg, wu, wdn, None, tm_moe, tf)
            final = l == depth - 1
            xp = _combine(pos_p, xp_mid, rp, g_final, y_sorted, final, tm_rows)
            xs = _combine(pos_s, xs_mid, rs, g_final, y_sorted, final, bs)

    nh = N_DIFF_HEADS
    return (xp.reshape(bp, sp, d), xs.reshape(bs, ss, d),
            jnp.stack(k_p).reshape(depth, bp, sp, nh, dv), jnp.stack(v_p).reshape(depth, bp, sp, nh, dv),
            jnp.stack(k_s).reshape(depth, bs, ss, nh, dv), jnp.stack(v_s).reshape(depth, bs, ss, nh, dv),
            jnp.stack(gv_s).reshape(depth, bs, ss, d_sgu))
```
